```python
import math
import jax, jax.numpy as jnp
from jax import lax
import numpy as np

D_MODEL = 1024
BATCH = 16
SEQ = 2048
DEPTH = 2

HEAD_DIM = 64
N_HEADS = D_MODEL // HEAD_DIM
A_HEADS = 4
B_HEADS = 6
C_HEADS = N_HEADS - A_HEADS - B_HEADS
A_WIDTH = A_HEADS * HEAD_DIM
B_WIDTH = B_HEADS * HEAD_DIM
C_WIDTH = C_HEADS * HEAD_DIM
A_QK_DIM = HEAD_DIM // 2
ROPE_THETA = 500000.0
ROPE_FRACTION = 4
DILATED_BRANCHES = ((128, 1), (512, 4), (2048, 16))
DENSE_Q_BLOCK = 128
MOBA_BLOCK = 256
MOBA_TOP_K = 3
MOBA_Q_CHUNK = 16
MEM_LEN = 256
MEM_HEADS = 4
MEM_HEAD_DIM = 64
MEM_WIDTH = MEM_HEADS * MEM_HEAD_DIM
D_FF = 2816
NORM_EPS = 1e-6
NEG_INF = -1e30
POS_OFFSET_MAX = 4096

kernel_name = "hymba_style_diff_dilated_moba_macaron"


def rms_norm(x, g):
    xf = x.astype(jnp.float32)
    y = xf * lax.rsqrt(jnp.mean(xf * xf, axis=-1, keepdims=True) + NORM_EPS)
    return (y * g.astype(jnp.float32)).astype(x.dtype)


def swiglu(h, w_gate, w_up, w_down):
    return (jax.nn.silu(h @ w_gate) * (h @ w_up)) @ w_down


def rotary_tables(positions, rot_dim):
    inv_freq = ROPE_THETA ** (-jnp.arange(0, rot_dim, 2, dtype=jnp.float32) / rot_dim)
    ang = positions.astype(jnp.float32)[..., None] * inv_freq
    return jnp.cos(ang), jnp.sin(ang)


def apply_partial_rotary(x, cos, sin):
    r = 2 * cos.shape[-1]
    c = cos[:, :, None, :].astype(x.dtype)
    s = sin[:, :, None, :].astype(x.dtype)
    x1, x2, xp = x[..., : r // 2], x[..., r // 2: r], x[..., r:]
    return jnp.concatenate([x1 * c - x2 * s, x2 * c + x1 * s, xp], axis=-1)


def differential_attention(q, k, v, lam, subln_g, lam_init):
    S = q.shape[1]
    dq = q.shape[-1] // 2
    scale = dq ** -0.5
    q1, q2 = q[..., :dq] * scale, q[..., dq:] * scale
    k1, k2 = k[..., :dq], k[..., dq:]
    outs = []
    for i in range(S // DENSE_Q_BLOCK):
        q_lo, q_hi = i * DENSE_Q_BLOCK, (i + 1) * DENSE_Q_BLOCK
        mask = jnp.arange(q_hi)[None, :] <= jnp.arange(q_lo, q_hi)[:, None]
        s1 = jnp.einsum('bqhd,bkhd->bhqk', q1[:, q_lo:q_hi], k1[:, :q_hi]).astype(jnp.float32)
        s2 = jnp.einsum('bqhd,bkhd->bhqk', q2[:, q_lo:q_hi], k2[:, :q_hi]).astype(jnp.float32)
        p1 = jax.nn.softmax(jnp.where(mask, s1, NEG_INF), axis=-1)
        p2 = jax.nn.softmax(jnp.where(mask, s2, NEG_INF), axis=-1)
        a = (p1 - lam * p2).astype(v.dtype)
        outs.append(jnp.einsum('bhqk,bkhd->bqhd', a, v[:, :q_hi]))
    o = jnp.concatenate(outs, axis=1)
    return rms_norm(o, subln_g) * (1.0 - lam_init)


def dilated_branch(q, k, v, window, dilation):
    B_, S, H, dh = q.shape
    L = S // dilation
    span = window // dilation
    Lp = -(-L // span) * span
    nb = Lp // span

    def to_sub(t):
        t = t.reshape(B_, L, dilation, H, dh).transpose(0, 2, 3, 1, 4)
        t = jnp.pad(t, ((0, 0), (0, 0), (0, 0), (0, Lp - L), (0, 0)))
        return t.reshape(B_, dilation, H, nb, span, dh)

    qb = to_sub(q) * dh ** -0.5
    kb, vb = to_sub(k), to_sub(v)
    prev = lambda t: jnp.pad(t, ((0, 0), (0, 0), (0, 0), (1, 0), (0, 0), (0, 0)))[:, :, :, :-1]
    kk = jnp.concatenate([prev(kb), kb], axis=4)
    vv = jnp.concatenate([prev(vb), vb], axis=4)
    s = jnp.einsum('brhnqd,brhnkd->brhnqk', qb, kk).astype(jnp.float32)
    qi = jnp.arange(span)[:, None] + span
    ki = jnp.arange(2 * span)[None, :]
    rel = qi - ki
    band = (rel >= 0) & (rel <= span)
    has_prev = (jnp.arange(nb) > 0)[:, None, None] | (ki >= span)[None]
    s = jnp.where(band[None] & has_prev, s, NEG_INF)
    lse = jax.nn.logsumexp(s, axis=-1)
    p = jnp.exp(s - lse[..., None]).astype(v.dtype)
    o = jnp.einsum('brhnqk,brhnkd->brhnqd', p, vv)
    o = o.reshape(B_, dilation, H, Lp, dh)[:, :, :, :L].transpose(0, 3, 1, 2, 4).reshape(B_, S, H, dh)
    lse = lse.reshape(B_, dilation, H, Lp)[..., :L].transpose(0, 3, 1, 2).reshape(B_, S, H)
    return o, lse


def dilated_attention(q, k, v):
    outs, lses = [], []
    for window, dilation in DILATED_BRANCHES:
        o, lse = dilated_branch(q, k, v, window, dilation)
        outs.append(o)
        lses.append(lse)
    w = jax.nn.softmax(jnp.stack(lses, axis=0), axis=0)
    o = jnp.einsum('nbsh,nbshd->bshd', w, jnp.stack(outs, axis=0).astype(jnp.float32))
    return o.astype(q.dtype)


def moba_attention(q, k, v):
    B_, S, H, dh = q.shape
    nblk = -(-S // MOBA_BLOCK)
    Sp = nblk * MOBA_BLOCK

    def prep(t):
        return jnp.pad(t, ((0, 0), (0, Sp - S), (0, 0), (0, 0))).transpose(0, 2, 1, 3)

    qh = prep(q) * dh ** -0.5
    kh, vh = prep(k), prep(v)
    qb = qh.reshape(B_, H, nblk, MOBA_BLOCK, dh)
    kb = kh.reshape(B_, H, nblk, MOBA_BLOCK, dh)
    vb = vh.reshape(B_, H, nblk, MOBA_BLOCK, dh)

    s_own = jnp.einsum('bhnqd,bhnkd->bhnqk', qb, kb).astype(jnp.float32)
    causal = jnp.tril(jnp.ones((MOBA_BLOCK, MOBA_BLOCK), dtype=bool))
    s_own = jnp.where(causal, s_own, NEG_INF)
    lse_own = jax.nn.logsumexp(s_own, axis=-1)
    o_own = jnp.einsum('bhnqk,bhnkd->bhnqd', jnp.exp(s_own - lse_own[..., None]).astype(v.dtype), vb)
    o_own = o_own.reshape(B_, H, Sp, dh)
    lse_own = lse_own.reshape(B_, H, Sp)

    k_mean = jnp.mean(kb.astype(jnp.float32), axis=3)
    gate = jnp.einsum('bhsd,bhnd->bhsn', qh.astype(jnp.float32), k_mean)
    q_blk = jnp.arange(Sp) // MOBA_BLOCK
    is_past = jnp.arange(nblk)[None, :] < q_blk[:, None]
    gate = jnp.where(is_past, gate, -jnp.inf)
    n_sel = min(MOBA_TOP_K, nblk)
    _, sel = lax.top_k(gate, n_sel)
    sel_ok = sel < q_blk[:, None]

    n_chunks = Sp // MOBA_Q_CHUNK

    def chunks(t):
        return jnp.moveaxis(t.reshape(B_, H, n_chunks, MOBA_Q_CHUNK, *t.shape[3:]), 2, 0)

    bi = jnp.arange(B_)[:, None, None, None]
    hi = jnp.arange(H)[None, :, None, None]

    def attend_selected(args):
        qc, sc, okc = args
        kg = kb[bi, hi, sc]
        vg = vb[bi, hi, sc]
        s = jnp.einsum('bhqd,bhqnkd->bhqnk', qc, kg).astype(jnp.float32)
        s = jnp.where(okc[..., None], s, NEG_INF).reshape(B_, H, MOBA_Q_CHUNK, n_sel * MOBA_BLOCK)
        lse = jax.nn.logsumexp(s, axis=-1)
        p = jnp.exp(s - lse[..., None]).reshape(B_, H, MOBA_Q_CHUNK, n_sel, MOBA_BLOCK)
        o = jnp.einsum('bhqnk,bhqnkd->bhqd', p.astype(vg.dtype), vg)
        return o, lse

    o_sel, lse_sel = lax.map(attend_selected, (chunks(qh), chunks(sel), chunks(sel_ok)))
    o_sel = jnp.moveaxis(o_sel, 0, 2).reshape(B_, H, Sp, dh)
    lse_sel = jnp.moveaxis(lse_sel, 0, 2).reshape(B_, H, Sp)

    m = jnp.maximum(lse_own, lse_sel)
    w_own = jnp.exp(lse_own - m)[..., None]
    w_sel = jnp.exp(lse_sel - m)[..., None]
    o = (w_own * o_own.astype(jnp.float32) + w_sel * o_sel.astype(jnp.float32)) / (w_own + w_sel)
    return o.astype(q.dtype).transpose(0, 2, 1, 3)[:, :S]


def memory_cross_attention(h, m, w_q, w_kv, w_o):
    B_, S, _ = h.shape
    M = m.shape[1]
    q = (h @ w_q).reshape(B_, S, MEM_HEADS, MEM_HEAD_DIM) * MEM_HEAD_DIM ** -0.5
    kv = (m @ w_kv).reshape(B_, M, 2, MEM_HEADS, MEM_HEAD_DIM)
    k, v = kv[:, :, 0], kv[:, :, 1]
    s = jnp.einsum('bshd,bmhd->bhsm', q, k).astype(jnp.float32)
    p = jax.nn.softmax(s, axis=-1).astype(v.dtype)
    o = jnp.einsum('bhsm,bmhd->bshd', p, v).reshape(B_, S, MEM_WIDTH)
    return o @ w_o


def setup_inputs(seed: int = 0) -> dict:
    key = jax.random.key(seed)
    ks = iter(jax.random.split(key, 40))
    L, D, F = DEPTH, D_MODEL, D_FF

    def nrm(shape, fan_in):
        return jax.random.normal(next(ks), shape, jnp.float32) * fan_in ** -0.5

    def gain(shape):
        return 1.0 + 0.01 * jax.random.normal(next(ks), shape, jnp.float32)

    x = jax.random.normal(next(ks), (BATCH, SEQ, D), jnp.float32)
    mem = jax.random.normal(next(ks), (BATCH, MEM_LEN, D), jnp.float32)
    positions = (jnp.arange(SEQ, dtype=jnp.int32)[None, :]
                 + jax.random.randint(next(ks), (BATCH, 1), 0, POS_OFFSET_MAX, dtype=jnp.int32))
    return {
        "x": x,
        "mem": mem,
        "positions": positions,
        "ffn1_norm": gain((L, D)),
        "ffn1_w_gate": nrm((L, D, F), D),
        "ffn1_w_up": nrm((L, D, F), D),
        "ffn1_w_down": nrm((L, F, D), F),
        "mix_norm": gain((L, D)),
        "w_in": nrm((L, D, 3 * D), D),
        "w_out": nrm((L, D, D), D),
        "diff_lambda_q1": 0.1 * jax.random.normal(next(ks), (L, A_QK_DIM), jnp.float32),
        "diff_lambda_k1": 0.1 * jax.random.normal(next(ks), (L, A_QK_DIM), jnp.float32),
        "diff_lambda_q2": 0.1 * jax.random.normal(next(ks), (L, A_QK_DIM), jnp.float32),
        "diff_lambda_k2": 0.1 * jax.random.normal(next(ks), (L, A_QK_DIM), jnp.float32),
        "diff_subln": gain((L, HEAD_DIM)),
        "memq_norm": gain((L, D)),
        "memkv_norm": gain((L, D)),
        "mem_w_q": nrm((L, D, MEM_WIDTH), D),
        "mem_w_kv": nrm((L, D, 2 * MEM_WIDTH), D),
        "mem_w_o": nrm((L, MEM_WIDTH, D), MEM_WIDTH),
        "ffn2_norm": gain((L, D)),
        "ffn2_w_gate": nrm((L, D, F), D),
        "ffn2_w_up": nrm((L, D, F), D),
        "ffn2_w_down": nrm((L, F, D), F),
        "final_norm": gain((D,)),
    }


def reference(x, mem, positions, ffn1_norm, ffn1_w_gate, ffn1_w_up, ffn1_w_down, mix_norm, w_in, w_out,
              diff_lambda_q1, diff_lambda_k1, diff_lambda_q2, diff_lambda_k2, diff_subln,
              memq_norm, memkv_norm, mem_w_q, mem_w_kv, mem_w_o,
              ffn2_norm, ffn2_w_gate, ffn2_w_up, ffn2_w_down, final_norm):
    B_, S, D = x.shape
    cos_a, sin_a = rotary_tables(positions, A_QK_DIM // ROPE_FRACTION)
    cos_h, sin_h = rotary_tables(positions, HEAD_DIM // ROPE_FRACTION)
    widths = (A_WIDTH, A_WIDTH, A_WIDTH, B_WIDTH, B_WIDTH, B_WIDTH, C_WIDTH, C_WIDTH, C_WIDTH)
    split_at = [sum(widths[:i + 1]) for i in range(len(widths) - 1)]

    for l in range(DEPTH):
        lam_init = 0.8 - 0.6 * math.exp(-0.3 * l)

        x = x + 0.5 * swiglu(rms_norm(x, ffn1_norm[l]), ffn1_w_gate[l], ffn1_w_up[l], ffn1_w_down[l])

        h = rms_norm(x, mix_norm[l])
        qa, ka, va, qb, kb, vb, qc, kc, vc = jnp.split(h @ w_in[l], split_at, axis=-1)

        qa = apply_partial_rotary(qa.reshape(B_, S, 2 * A_HEADS, A_QK_DIM), cos_a, sin_a).reshape(B_, S, A_HEADS, HEAD_DIM)
        ka = apply_partial_rotary(ka.reshape(B_, S, 2 * A_HEADS, A_QK_DIM), cos_a, sin_a).reshape(B_, S, A_HEADS, HEAD_DIM)
        va = va.reshape(B_, S, A_HEADS, HEAD_DIM)
        lam = (jnp.exp(jnp.sum(diff_lambda_q1[l].astype(jnp.float32) * diff_lambda_k1[l].astype(jnp.float32)))
               - jnp.exp(jnp.sum(diff_lambda_q2[l].astype(jnp.float32) * diff_lambda_k2[l].astype(jnp.float32)))
               + lam_init)
        o_a = differential_attention(qa, ka, va, lam, diff_subln[l], lam_init)

        qb = apply_partial_rotary(qb.reshape(B_, S, B_HEADS, HEAD_DIM), cos_h, sin_h)
        kb = apply_partial_rotary(kb.reshape(B_, S, B_HEADS, HEAD_DIM), cos_h, sin_h)
        o_b = dilated_attention(qb, kb, vb.reshape(B_, S, B_HEADS, HEAD_DIM))

        qc = apply_partial_rotary(qc.reshape(B_, S, C_HEADS, HEAD_DIM), cos_h, sin_h)
        kc = apply_partial_rotary(kc.reshape(B_, S, C_HEADS, HEAD_DIM), cos_h, sin_h)
        o_c = moba_attention(qc, kc, vc.reshape(B_, S, C_HEADS, HEAD_DIM))

        mixed = jnp.concatenate([o_a.reshape(B_, S, A_WIDTH), o_b.reshape(B_, S, B_WIDTH),
                                 o_c.reshape(B_, S, C_WIDTH)], axis=-1)
        x = x + mixed @ w_out[l]

        x = x + memory_cross_attention(rms_norm(x, memq_norm[l]), rms_norm(mem, memkv_norm[l]),
                                       mem_w_q[l], mem_w_kv[l], mem_w_o[l])

        x = x + 0.5 * swiglu(rms_norm(x, ffn2_norm[l]), ffn2_w_gate[l], ffn2_w_up[l], ffn2_w_down[l])

    return rms_norm(x, final_norm)
```

```python
import functools
import math

import jax
import jax.numpy as jnp
from jax import lax
from jax.experimental import pallas as pl
from jax.experimental.pallas import tpu as pltpu

F32 = jnp.float32
BF16 = jnp.bfloat16

D_MODEL = 1024
HEAD_DIM = 64
A_HEADS, B_HEADS, C_HEADS = 4, 6, 6
A_QK_DIM = HEAD_DIM // 2
ROPE_THETA = 500000.0
ROPE_FRACTION = 4
MOBA_BLOCK = 256
MOBA_TOP_K = 3
MEM_HEADS = 4
MEM_WIDTH = MEM_HEADS * HEAD_DIM
D_FF = 2816
NORM_EPS = 1e-6
NEG = -1e30

LANES = 128
VMEM_LIMIT = 56 * 1024 * 1024

TM = 512
FF_CHUNK = 512
PROJ_CHUNK = 512
TQ = 256
TK = 256

A_Q0, A_K0, A_V0 = 0, 2, 4
B_Q0, B_K0, B_V0 = 6, 9, 12
C_Q0, C_K0, C_V0 = 15, 18, 21
N_COL_BLOCKS = 24


def _col_kind(cb):
    if cb < 4:
        return "A", (A_QK_DIM ** -0.5 if cb < 2 else 1.0)
    if 6 <= cb < 12:
        return "H", (HEAD_DIM ** -0.5 if cb < 9 else 1.0)
    if 15 <= cb < 21:
        return "H", (HEAD_DIM ** -0.5 if cb < 18 else 1.0)
    return None, 1.0


def _rms(x, g):
    return x * lax.rsqrt(jnp.mean(x * x, axis=-1, keepdims=True) + NORM_EPS) * g


def _resident(shape):
    nd = len(shape)
    return pl.BlockSpec(shape, lambda *_: (0,) * nd)


def _params(*sem):
    return pltpu.CompilerParams(dimension_semantics=sem, vmem_limit_bytes=VMEM_LIMIT)


def _ffn_body(*refs, final):
    if final:
        x_ref, g_ref, wg_ref, wu_ref, wd_ref, fg_ref, o_ref = refs
    else:
        x_ref, g_ref, wg_ref, wu_ref, wd_ref, o_ref = refs
    x = x_ref[...]
    h = _rms(x, g_ref[...]).astype(BF16)
    acc = jnp.zeros(x.shape, F32)
    for c0 in range(0, D_FF, FF_CHUNK):
        c1 = min(c0 + FF_CHUNK, D_FF)
        a = jnp.dot(h, wg_ref[:, c0:c1], preferred_element_type=F32)
        u = jnp.dot(h, wu_ref[:, c0:c1], preferred_element_type=F32)
        t = (a / (1.0 + jnp.exp(-a))) * u
        acc = acc + jnp.dot(t.astype(BF16), wd_ref[c0:c1, :], preferred_element_type=F32)
    y = x + 0.5 * acc
    if final:
        y = _rms(y, fg_ref[...])
    o_ref[...] = y


def _ffn(x, g, wg, wu, wd, final_g=None):
    t, d = x.shape
    final = final_g is not None
    in_specs = [
        pl.BlockSpec((TM, d), lambda i: (i, 0)),
        _resident((1, d)),
        _resident((d, D_FF)),
        _resident((d, D_FF)),
        _resident((D_FF, d)),
    ]
    args = [x, g, wg, wu, wd]
    if final:
        in_specs.append(_resident((1, d)))
        args.append(final_g)
    return pl.pallas_call(
        functools.partial(_ffn_body, final=final),
        out_shape=jax.ShapeDtypeStruct((t, d), F32),
        grid=(t // TM,),
        in_specs=in_specs,
        out_specs=pl.BlockSpec((TM, d), lambda i: (i, 0)),
        compiler_params=_params("parallel"),
        name="ffn_final" if final else "ffn",
    )(*args)


def _proj_body(x_ref, g_ref, w_ref, ca_ref, sa_ref, ch_ref, sh_ref, o_ref):
    h = _rms(x_ref[...], g_ref[...]).astype(BF16)
    tm = h.shape[0]
    lane = lax.broadcasted_iota(jnp.int32, (tm, LANES), 1)
    half_a = A_QK_DIM // ROPE_FRACTION // 2
    half_h = HEAD_DIM // ROPE_FRACTION // 2
    first_a = (lane & (A_QK_DIM - 1)) < half_a
    first_h = (lane & (HEAD_DIM - 1)) < half_h
    per_chunk = PROJ_CHUNK // LANES
    for c in range(N_COL_BLOCKS // per_chunk):
        y = jnp.dot(h, w_ref[:, c * PROJ_CHUNK:(c + 1) * PROJ_CHUNK], preferred_element_type=F32)
        for k in range(per_chunk):
            cb = c * per_chunk + k
            blk = y[:, k * LANES:(k + 1) * LANES]
            kind, scale = _col_kind(cb)
            if kind == "A":
                partner = jnp.where(first_a, pltpu.roll(blk, LANES - half_a, 1), pltpu.roll(blk, half_a, 1))
                blk = blk * ca_ref[...] + partner * sa_ref[...]
            elif kind == "H":
                partner = jnp.where(first_h, pltpu.roll(blk, LANES - half_h, 1), pltpu.roll(blk, half_h, 1))
                blk = blk * ch_ref[...] + partner * sh_ref[...]
            if scale != 1.0:
                blk = blk * scale
            o_ref[:, cb * LANES:(cb + 1) * LANES] = blk.astype(BF16)


def _proj(x, g, w_in, ca, sa, ch, sh):
    t, d = x.shape
    n = w_in.shape[1]
    tab = pl.BlockSpec((TM, LANES), lambda i: (i, 0))
    return pl.pallas_call(
        _proj_body,
        out_shape=jax.ShapeDtypeStruct((t, n), BF16),
        grid=(t // TM,),
        in_specs=[pl.BlockSpec((TM, d), lambda i: (i, 0)), _resident((1, d)), _resident((d, n)),
                  tab, tab, tab, tab],
        out_specs=pl.BlockSpec((TM, n), lambda i: (i, 0)),
        compiler_params=_params("parallel"),
        name="mix_proj",
    )(x, g, w_in, ca, sa, ch, sh)


def _attn_body(*refs, mode, nmaps, lam_init):
    if mode == "diff":
        q_ref, k_ref, v_ref, lam_ref, sg_ref, o_ref, m_scr, l_scr, acc_scr = refs
    elif mode == "moba":
        q_ref, k_ref, v_ref, o_ref, m_scr, l_scr, acc_scr, km_scr = refs
    else:
        q_ref, k_ref, v_ref, o_ref, m_scr, l_scr, acc_scr = refs
    i = pl.program_id(2)
    mrows = nmaps * TQ
    width = LANES // nmaps

    lane_q = lax.broadcasted_iota(jnp.int32, (TQ, LANES), 1)
    q32 = q_ref[0].astype(F32)
    qm = jnp.concatenate(
        [jnp.where((lane_q >= m * width) & (lane_q < (m + 1) * width), q32, 0.0) for m in range(nmaps)],
        axis=0).astype(BF16)

    if mode == "moba":
        nblk = k_ref.shape[1] // MOBA_BLOCK

        @pl.when(i == 0)
        def _():
            km_scr[...] = jnp.zeros(km_scr.shape, F32)
            for j in range(nblk):
                kb = k_ref[0, j * MOBA_BLOCK:(j + 1) * MOBA_BLOCK, :].astype(F32)
                km_scr[j:j + 1, :] = jnp.mean(kb, axis=0, keepdims=True)

        km = km_scr[...]
        km_hi = km.astype(BF16)
        km_lo = (km - km_hi.astype(F32)).astype(BF16)
        nt = (((1,), (1,)), ((), ()))
        gate = (lax.dot_general(qm, km_hi, nt, preferred_element_type=F32)
                + lax.dot_general(qm, km_lo, nt, preferred_element_type=F32))
        lane_g = lax.broadcasted_iota(jnp.int32, (mrows, LANES), 1)
        rank = jnp.zeros((mrows, LANES), F32)
        for jp in range(nblk):
            col = gate[:, jp:jp + 1]
            beats = (col > gate) | ((col == gate) & (lane_g > jp))
            rank = rank + jnp.where(beats, 1.0, 0.0) * jnp.where(jp < i, 1.0, 0.0)
        sel = jnp.where((rank < MOBA_TOP_K - 0.5) & (lane_g < i), 1.0, 0.0)

    m_scr[...] = jnp.full(m_scr.shape, NEG, F32)
    l_scr[...] = jnp.zeros(l_scr.shape, F32)
    acc_scr[...] = jnp.zeros(acc_scr.shape, F32)

    row = lax.broadcasted_iota(jnp.int32, (mrows, TK), 0)
    col = lax.broadcasted_iota(jnp.int32, (mrows, TK), 1)
    delta0 = (row & (TQ - 1)) - col

    def step(t, carry):
        j = i - t
        k = k_ref[0, pl.ds(pl.multiple_of(j * TK, TK), TK), :]
        v = v_ref[0, pl.ds(pl.multiple_of(j * TK, TK), TK), :]
        s = lax.dot_general(qm, k, (((1,), (1,)), ((), ())), preferred_element_type=F32)
        delta = delta0 + t * TK
        if mode == "dil":
            cnt = (jnp.where(delta <= 128, 1.0, 0.0)
                   + jnp.where(((delta & 3) == 0) & (delta <= 512), 1.0, 0.0)
                   + jnp.where((delta & 15) == 0, 1.0, 0.0))
            cnt = jnp.where(delta >= 0, cnt, 0.0)
            s = jnp.where(cnt > 0.0, s, NEG)
        elif mode == "moba":
            selcol = jnp.max(jnp.where(lane_g == j, sel, 0.0), axis=1, keepdims=True)
            selcol = jnp.maximum(selcol, jnp.where(t == 0, 1.0, 0.0))
            s = jnp.where((delta >= 0) & (selcol > 0.5), s, NEG)
        else:
            s = jnp.where(delta >= 0, s, NEG)
        m_old = m_scr[...]
        m_new = jnp.maximum(m_old, jnp.max(s, axis=1, keepdims=True))
        alpha = jnp.exp(m_old - m_new)
        p = jnp.exp(s - m_new)
        if mode == "dil":
            p = p * cnt
        l_scr[...] = alpha * l_scr[...] + jnp.sum(p, axis=1, keepdims=True)
        acc_scr[...] = alpha * acc_scr[...] + jnp.dot(p.astype(BF16), v, preferred_element_type=F32)
        m_scr[...] = m_new
        return carry

    lax.fori_loop(0, i + 1, step, 0)

    o = acc_scr[...] / l_scr[...]
    if mode == "diff":
        lam = lam_ref[...]
        o = jnp.where(lane_q < HEAD_DIM,
                      o[0:TQ] - lam * o[TQ:2 * TQ],
                      o[2 * TQ:3 * TQ] - lam * o[3 * TQ:4 * TQ])
        sq = o * o
        ss0 = jnp.sum(jnp.where(lane_q < HEAD_DIM, sq, 0.0), axis=1, keepdims=True)
        ss1 = jnp.sum(jnp.where(lane_q >= HEAD_DIM, sq, 0.0), axis=1, keepdims=True)
        var = jnp.where(lane_q < HEAD_DIM, ss0, ss1) * (1.0 / HEAD_DIM)
        o = o * lax.rsqrt(var + NORM_EPS) * sg_ref[...] * (1.0 - lam_init)
    else:
        o = jnp.where(lane_q < HEAD_DIM, o[0:TQ], o[TQ:2 * TQ])
    o_ref[0] = o.astype(BF16)


def _attn(qkv, *, mode, ngroups, q0, k0, v0, extra=(), lam_init=0.0):
    b, s, _ = qkv.shape
    nmaps = 4 if mode == "diff" else 2
    mrows = nmaps * TQ
    in_specs = [
        pl.BlockSpec((1, TQ, LANES), lambda bi, g, i: (bi, i, q0 + g)),
        pl.BlockSpec((1, s, LANES), lambda bi, g, i: (bi, 0, k0 + g)),
        pl.BlockSpec((1, s, LANES), lambda bi, g, i: (bi, 0, v0 + g)),
    ] + [_resident((1, LANES)) for _ in extra]
    scratch = [pltpu.VMEM((mrows, 1), F32), pltpu.VMEM((mrows, 1), F32), pltpu.VMEM((mrows, LANES), F32)]
    if mode == "moba":
        scratch.append(pltpu.VMEM((LANES, LANES), F32))
    return pl.pallas_call(
        functools.partial(_attn_body, mode=mode, nmaps=nmaps, lam_init=lam_init),
        out_shape=jax.ShapeDtypeStruct((b, s, ngroups * LANES), BF16),
        grid=(b, ngroups, s // TQ),
        in_specs=in_specs,
        out_specs=pl.BlockSpec((1, TQ, LANES), lambda bi, g, i: (bi, i, g)),
        scratch_shapes=scratch,
        compiler_params=_params("parallel", "parallel", "arbitrary"),
        name="attn_" + mode,
    )(qkv, qkv, qkv, *extra)


def _memkv_body(m_ref, g_ref, w_ref, o_ref):
    h = _rms(m_ref[0], g_ref[...]).astype(BF16)
    o_ref[0] = jnp.dot(h, w_ref[...], preferred_element_type=F32).astype(BF16)


def _memkv(mem, g, w_kv):
    b, m, d = mem.shape
    n = w_kv.shape[1]
    return pl.pallas_call(
        _memkv_body,
        out_shape=jax.ShapeDtypeStruct((b, m, n), BF16),
        grid=(b,),
        in_specs=[pl.BlockSpec((1, m, d), lambda i: (i, 0, 0)), _resident((1, d)), _resident((d, n))],
        out_specs=pl.BlockSpec((1, m, n), lambda i: (i, 0, 0)),
        compiler_params=_params("parallel"),
        name="mem_kv",
    )(mem, g, w_kv)


def _post_body(x_ref, oa_ref, ob_ref, oc_ref, woa_ref, wob_ref, woc_ref,
               g_ref, wq_ref, kv_ref, wo_ref, o_ref):
    x = x_ref[...]
    x = x + jnp.dot(oa_ref[...], woa_ref[...], preferred_element_type=F32)
    x = x + jnp.dot(ob_ref[...], wob_ref[...], preferred_element_type=F32)
    x = x + jnp.dot(oc_ref[...], woc_ref[...], preferred_element_type=F32)

    h = _rms(x, g_ref[...]).astype(BF16)
    q = jnp.dot(h, wq_ref[...], preferred_element_type=F32) * (HEAD_DIM ** -0.5)
    tm = q.shape[0]
    lane = lax.broadcasted_iota(jnp.int32, (tm, LANES), 1)
    outs = []
    for gi in range(MEM_WIDTH // LANES):
        qg = q[:, gi * LANES:(gi + 1) * LANES]
        qm = jnp.concatenate([jnp.where(lane < HEAD_DIM, qg, 0.0),
                              jnp.where(lane >= HEAD_DIM, qg, 0.0)], axis=0).astype(BF16)
        kg = kv_ref[0, :, gi * LANES:(gi + 1) * LANES]
        vg = kv_ref[0, :, MEM_WIDTH + gi * LANES:MEM_WIDTH + (gi + 1) * LANES]
        s = lax.dot_general(qm, kg, (((1,), (1,)), ((), ())), preferred_element_type=F32)
        p = jnp.exp(s - jnp.max(s, axis=1, keepdims=True))
        l = jnp.sum(p, axis=1, keepdims=True)
        o = jnp.dot(p.astype(BF16), vg, preferred_element_type=F32) / l
        outs.append(jnp.where(lane < HEAD_DIM, o[0:tm], o[tm:2 * tm]).astype(BF16))
    o_mem = jnp.concatenate(outs, axis=1)
    o_ref[...] = x + jnp.dot(o_mem, wo_ref[...], preferred_element_type=F32)


def _post(x, oa, ob, oc, woa, wob, woc, g, wq, kv, wo, seq):
    t, d = x.shape
    tiles_per_seq = seq // TM
    tile = lambda w: pl.BlockSpec((TM, w), lambda i: (i, 0))
    m, n = kv.shape[1], kv.shape[2]
    return pl.pallas_call(
        _post_body,
        out_shape=jax.ShapeDtypeStruct((t, d), F32),
        grid=(t // TM,),
        in_specs=[tile(d), tile(oa.shape[1]), tile(ob.shape[1]), tile(oc.shape[1]),
                  _resident(woa.shape), _resident(wob.shape), _resident(woc.shape),
                  _resident((1, d)), _resident(wq.shape),
                  pl.BlockSpec((1, m, n), lambda i: (i // tiles_per_seq, 0, 0)),
                  _resident(wo.shape)],
        out_specs=tile(d),
        compiler_params=_params("parallel"),
        name="mix_out_mem",
    )(x, oa, ob, oc, woa, wob, woc, g, wq, kv, wo)


def _rotary_tables(positions, head_width):
    rot = head_width // ROPE_FRACTION
    inv_freq = ROPE_THETA ** (-jnp.arange(0, rot, 2, dtype=F32) / rot)
    ang = positions.astype(F32).reshape(-1, 1) * inv_freq
    cos, sin = jnp.cos(ang), jnp.sin(ang)
    t = ang.shape[0]
    rest = head_width - rot
    c_unit = jnp.concatenate([cos, cos, jnp.ones((t, rest), F32)], axis=-1)
    s_unit = jnp.concatenate([-sin, sin, jnp.zeros((t, rest), F32)], axis=-1)
    reps = LANES // head_width
    return jnp.tile(c_unit, (1, reps)), jnp.tile(s_unit, (1, reps))


def kernel(x, mem, positions, ffn1_norm, ffn1_w_gate, ffn1_w_up, ffn1_w_down, mix_norm, w_in, w_out,
           diff_lambda_q1, diff_lambda_k1, diff_lambda_q2, diff_lambda_k2, diff_subln,
           memq_norm, memkv_norm, mem_w_q, mem_w_kv, mem_w_o,
           ffn2_norm, ffn2_w_gate, ffn2_w_up, ffn2_w_down, final_norm):
    b, s, d = x.shape
    depth = w_in.shape[0]
    assert d == D_MODEL and s % TQ == 0 and s % TM == 0 and TQ == MOBA_BLOCK
    a_w, b_w = A_HEADS * HEAD_DIM, B_HEADS * HEAD_DIM

    ca, sa = _rotary_tables(positions, A_QK_DIM)
    ch, sh = _rotary_tables(positions, HEAD_DIM)
    row = lambda v: v.reshape(1, -1).astype(F32)
    bf = lambda w: w.astype(BF16)

    xt = x.reshape(b * s, d)
    for l in range(depth):
        lam_init = 0.8 - 0.6 * math.exp(-0.3 * l)
        lam = (jnp.exp(jnp.sum(diff_lambda_q1[l].astype(F32) * diff_lambda_k1[l].astype(F32)))
               - jnp.exp(jnp.sum(diff_lambda_q2[l].astype(F32) * diff_lambda_k2[l].astype(F32)))
               + lam_init)
        lam_row = jnp.full((1, LANES), lam, F32)
        subln_row = jnp.tile(row(diff_subln[l]), (1, LANES // HEAD_DIM))

        xt = _ffn(xt, row(ffn1_norm[l]), bf(ffn1_w_gate[l]), bf(ffn1_w_up[l]), bf(ffn1_w_down[l]))

        qkv = _proj(xt, row(mix_norm[l]), bf(w_in[l]), ca, sa, ch, sh).reshape(b, s, -1)
        oa = _attn(qkv, mode="diff", ngroups=2, q0=A_Q0, k0=A_K0, v0=A_V0,
                   extra=(lam_row, subln_row), lam_init=lam_init)
        ob = _attn(qkv, mode="dil", ngroups=3, q0=B_Q0, k0=B_K0, v0=B_V0)
        oc = _attn(qkv, mode="moba", ngroups=3, q0=C_Q0, k0=C_K0, v0=C_V0)

        kv = _memkv(mem, row(memkv_norm[l]), bf(mem_w_kv[l]))
        wo = bf(w_out[l])
        xt = _post(xt, oa.reshape(b * s, -1), ob.reshape(b * s, -1), oc.reshape(b * s, -1),
                   wo[:a_w], wo[a_w:a_w + b_w], wo[a_w + b_w:],
                   row(memq_norm[l]), bf(mem_w_q[l]), kv, bf(mem_w_o[l]), s)

        last = l == depth - 1
        xt = _ffn(xt, row(ffn2_norm[l]), bf(ffn2_w_gate[l]), bf(ffn2_w_up[l]), bf(ffn2_w_down[l]),
                  final_g=row(final_norm) if last else None)
    return xt.reshape(b, s, d)
```

```python
import functools
import math

import jax
import jax.numpy as jnp
from jax import lax
from jax.experimental import pallas as pl
from jax.experimental.pallas import tpu as pltpu

F32 = jnp.float32
BF16 = jnp.bfloat16

D_MODEL = 1024
HEAD_DIM = 64
A_HEADS, B_HEADS, C_HEADS = 4, 6, 6
A_QK_DIM = HEAD_DIM // 2
ROPE_THETA = 500000.0
ROPE_FRACTION = 4
DILATED_BRANCHES = ((128, 1), (512, 4), (2048, 16))
MOBA_BLOCK = 256
MOBA_TOP_K = 3
MEM_HEADS = 4
MEM_WIDTH = MEM_HEADS * HEAD_DIM
D_FF = 2816
NORM_EPS = 1e-6
NEG = -1e30

LANES = 128
BF16_SUBLANES = 16
VMEM_LIMIT = 56 * 1024 * 1024

TM = 512
FF_CHUNK = 512
PROJ_CHUNK = 512
TQ = 256
TK = 256
N_GROUPS = D_MODEL // LANES
A_GROUPS, B_GROUPS, C_GROUPS = 2, 3, 3


def _group_kind(g):
    if g < A_GROUPS:
        return "A", A_QK_DIM ** -0.5
    return "H", HEAD_DIM ** -0.5


def _rms(x, g):
    return x * lax.rsqrt(jnp.mean(x * x, axis=-1, keepdims=True) + NORM_EPS) * g


def _resident(shape):
    nd = len(shape)
    return pl.BlockSpec(shape, lambda *_: (0,) * nd)


def _params(*sem):
    return pltpu.CompilerParams(dimension_semantics=sem, vmem_limit_bytes=VMEM_LIMIT)


def _ffn_body(*refs, final):
    if final:
        x_ref, g_ref, wg_ref, wu_ref, wd_ref, fg_ref, o_ref = refs
    else:
        x_ref, g_ref, wg_ref, wu_ref, wd_ref, o_ref = refs
    x = x_ref[...]
    h = _rms(x, g_ref[...]).astype(BF16)
    acc = jnp.zeros(x.shape, F32)
    for c0 in range(0, D_FF, FF_CHUNK):
        c1 = min(c0 + FF_CHUNK, D_FF)
        a = jnp.dot(h, wg_ref[:, c0:c1], preferred_element_type=F32)
        u = jnp.dot(h, wu_ref[:, c0:c1], preferred_element_type=F32)
        t = (a / (1.0 + jnp.exp(-a))) * u
        acc = acc + jnp.dot(t.astype(BF16), wd_ref[c0:c1, :], preferred_element_type=F32)
    y = x + 0.5 * acc
    if final:
        y = _rms(y, fg_ref[...])
    o_ref[...] = y


def _ffn(x, g, wg, wu, wd, final_g=None):
    t, d = x.shape
    final = final_g is not None
    in_specs = [
        pl.BlockSpec((TM, d), lambda i: (i, 0)),
        _resident((1, d)),
        _resident((d, D_FF)),
        _resident((d, D_FF)),
        _resident((D_FF, d)),
    ]
    args = [x, g, wg, wu, wd]
    if final:
        in_specs.append(_resident((1, d)))
        args.append(final_g)
    return pl.pallas_call(
        functools.partial(_ffn_body, final=final),
        out_shape=jax.ShapeDtypeStruct((t, d), F32),
        grid=(t // TM,),
        in_specs=in_specs,
        out_specs=pl.BlockSpec((TM, d), lambda i: (i, 0)),
        compiler_params=_params("parallel"),
        name="ffn_final" if final else "ffn",
    )(*args)


def _rotate_half(blk, first, half, axis):
    n = blk.shape[axis]
    return jnp.where(first, pltpu.roll(blk, n - half, axis), pltpu.roll(blk, half, axis))


def _proj_body(x_ref, g_ref, wk_ref, wqt_ref, wvt_ref, ca_ref, sa_ref, ch_ref, sh_ref,
               cat_ref, sat_ref, cht_ref, sht_ref, k_ref, qt_ref, vt_ref):
    h = _rms(x_ref[0], g_ref[...]).astype(BF16)
    tm = h.shape[0]
    half = {"A": A_QK_DIM // ROPE_FRACTION // 2, "H": HEAD_DIM // ROPE_FRACTION // 2}
    width = {"A": A_QK_DIM, "H": HEAD_DIM}
    nt = (((1,), (1,)), ((), ()))
    per_chunk = PROJ_CHUNK // LANES

    lane = lax.broadcasted_iota(jnp.int32, (tm, LANES), 1)
    first_n = {k: (lane & (width[k] - 1)) < half[k] for k in ("A", "H")}
    tab_n = {"A": (ca_ref, sa_ref), "H": (ch_ref, sh_ref)}
    for c in range(N_GROUPS // per_chunk):
        y = jnp.dot(h, wk_ref[:, c * PROJ_CHUNK:(c + 1) * PROJ_CHUNK], preferred_element_type=F32)
        for j in range(per_chunk):
            g = c * per_chunk + j
            kind, _ = _group_kind(g)
            blk = y[:, j * LANES:(j + 1) * LANES]
            cos_ref, sin_ref = tab_n[kind]
            blk = blk * cos_ref[...] + _rotate_half(blk, first_n[kind], half[kind], 1) * sin_ref[...]
            k_ref[0, :, g * LANES:(g + 1) * LANES] = blk.astype(BF16)

    sub = lax.broadcasted_iota(jnp.int32, (LANES, tm), 0)
    first_t = {k: (sub & (width[k] - 1)) < half[k] for k in ("A", "H")}
    tab_t = {"A": (cat_ref, sat_ref), "H": (cht_ref, sht_ref)}
    for c in range(N_GROUPS // per_chunk):
        y = lax.dot_general(wqt_ref[c * PROJ_CHUNK:(c + 1) * PROJ_CHUNK, :], h, nt,
                            preferred_element_type=F32)
        for j in range(per_chunk):
            g = c * per_chunk + j
            kind, scale = _group_kind(g)
            blk = y[j * LANES:(j + 1) * LANES, :]
            cos_ref, sin_ref = tab_t[kind]
            blk = blk * cos_ref[...] + _rotate_half(blk, first_t[kind], half[kind], 0) * sin_ref[...]
            blk = (blk * scale).astype(BF16)
            for t in range(tm // TQ):
                qt_ref[0, t, g * LANES:(g + 1) * LANES, :] = blk[:, t * TQ:(t + 1) * TQ]

    for c in range(N_GROUPS // per_chunk):
        y = lax.dot_general(wvt_ref[c * PROJ_CHUNK:(c + 1) * PROJ_CHUNK, :], h, nt,
                            preferred_element_type=F32).astype(BF16)
        for t in range(tm // TQ):
            vt_ref[0, t, c * PROJ_CHUNK:(c + 1) * PROJ_CHUNK, :] = y[:, t * TQ:(t + 1) * TQ]


def _proj(x, g, wk, wqt, wvt, tabs_n, tabs_t):
    b, s, d = x.shape
    per_seq = s // TM
    tab_n = pl.BlockSpec((TM, LANES), lambda bi, i: (bi * per_seq + i, 0))
    tab_t = pl.BlockSpec((LANES, TM), lambda bi, i: (0, bi * per_seq + i))
    slab = pl.BlockSpec((1, TM // TQ, d, TQ), lambda bi, i: (bi, i, 0, 0))
    return pl.pallas_call(
        _proj_body,
        out_shape=(jax.ShapeDtypeStruct((b, s, d), BF16),
                   jax.ShapeDtypeStruct((b, s // TQ, d, TQ), BF16),
                   jax.ShapeDtypeStruct((b, s // TQ, d, TQ), BF16)),
        grid=(b, per_seq),
        in_specs=[pl.BlockSpec((1, TM, d), lambda bi, i: (bi, i, 0)), _resident((1, d)),
                  _resident((d, d)), _resident((d, d)), _resident((d, d)),
                  tab_n, tab_n, tab_n, tab_n, tab_t, tab_t, tab_t, tab_t],
        out_specs=(pl.BlockSpec((1, TM, d), lambda bi, i: (bi, i, 0)), slab, slab),
        compiler_params=_params("parallel", "parallel"),
        name="mix_proj",
    )(x, g, wk, wqt, wvt, *tabs_n, *tabs_t)


def _dilated_bias(offset):
    r = lax.broadcasted_iota(jnp.int32, (TK, TQ), 0)
    c = lax.broadcasted_iota(jnp.int32, (TK, TQ), 1)
    delta = c - r + offset * TK
    cnt = jnp.zeros((TK, TQ), jnp.int32)
    for window, dilation in DILATED_BRANCHES:
        hit = (delta <= window) & ((delta & (dilation - 1)) == 0)
        cnt = cnt + jnp.where(hit, 1, 0)
    cnt = jnp.where(delta >= 0, cnt, 0)
    return jnp.where(cnt == 3, math.log(3.0),
                     jnp.where(cnt == 2, math.log(2.0), jnp.where(cnt == 1, 0.0, NEG))).astype(F32)


N_BIAS = 4


def _attn_body(*refs, mode, nmaps, lam_init):
    if mode == "diff":
        qt_ref, k_ref, vt_ref, lam_ref, sg_ref, o_ref, qm_scr, m_scr, l_scr, acc_scr = refs
    elif mode == "moba":
        qt_ref, k_ref, vt_ref, o_ref, qm_scr, m_scr, l_scr, acc_scr, km_scr, sel_scr = refs
    else:
        qt_ref, k_ref, vt_ref, o_ref, qm_scr, m_scr, l_scr, acc_scr, bias_scr = refs
    i = pl.program_id(2)
    mcols = nmaps * TQ
    width = LANES // nmaps

    sub_q = lax.broadcasted_iota(jnp.int32, (LANES, TQ), 0)
    q32 = qt_ref[0, 0].astype(F32)
    for m in range(nmaps):
        keep = (sub_q >= m * width) & (sub_q < (m + 1) * width)
        qm_scr[:, m * TQ:(m + 1) * TQ] = jnp.where(keep, q32, 0.0).astype(BF16)

    if mode == "dil":
        @pl.when((pl.program_id(0) == 0) & (pl.program_id(1) == 0) & (i == 0))
        def _():
            for off in range(N_BIAS):
                bias_scr[off] = _dilated_bias(off)

    if mode == "moba":
        nblk = k_ref.shape[1] // MOBA_BLOCK

        @pl.when(i == 0)
        def _():
            km_scr[...] = jnp.zeros(km_scr.shape, F32)
            for j in range(nblk):
                kb = k_ref[0, j * MOBA_BLOCK:(j + 1) * MOBA_BLOCK, :].astype(F32)
                km_scr[j:j + 1, :] = jnp.mean(kb, axis=0, keepdims=True)

        km = km_scr[...]
        km_hi = km.astype(BF16)
        km_lo = (km - km_hi.astype(F32)).astype(BF16)
        qm = qm_scr[...]
        gate = (jnp.dot(km_hi, qm, preferred_element_type=F32)
                + jnp.dot(km_lo, qm, preferred_element_type=F32))
        blk_id = lax.broadcasted_iota(jnp.int32, gate.shape, 0)
        rank = jnp.zeros(gate.shape, F32)
        for jp in range(nblk):
            rowj = gate[jp:jp + 1, :]
            beats = (rowj > gate) | ((rowj == gate) & (blk_id > jp))
            rank = rank + jnp.where(beats, 1.0, 0.0) * jnp.where(jp < i, 1.0, 0.0)
        sel_scr[...] = jnp.where((rank < MOBA_TOP_K - 0.5) & (blk_id < i), 1.0, 0.0)

    m_scr[...] = jnp.full(m_scr.shape, NEG, F32)
    l_scr[...] = jnp.zeros(l_scr.shape, F32)
    acc_scr[...] = jnp.zeros(acc_scr.shape, F32)

    def tile_update(j, t, diagonal):
        k = k_ref[0, pl.ds(pl.multiple_of(j * TK, TK), TK), :]
        vt = vt_ref[0, j]
        s = jnp.dot(k, qm_scr[...], preferred_element_type=F32)
        if mode == "dil":
            bias = bias_scr[jnp.minimum(t, N_BIAS - 1)]
            s = s + jnp.concatenate([bias] * nmaps, axis=1)
        elif diagonal:
            kpos = lax.broadcasted_iota(jnp.int32, (TK, mcols), 0)
            qpos = lax.broadcasted_iota(jnp.int32, (TK, mcols), 1) & (TQ - 1)
            s = jnp.where(kpos <= qpos, s, NEG)
        m_old = m_scr[...]
        m_tile = jnp.max(s, axis=0, keepdims=True)
        if mode == "moba" and not diagonal:
            chosen = sel_scr[pl.ds(j, 1), :] > 0.5
            m_new = jnp.maximum(m_old, jnp.where(chosen, m_tile, NEG))
            m_sub = jnp.where(chosen, m_new, -NEG)
        else:
            m_new = jnp.maximum(m_old, m_tile)
            m_sub = m_new
        alpha = jnp.exp(m_old - m_new)
        p = jnp.exp(s - m_sub)
        l_scr[...] = alpha * l_scr[...] + jnp.sum(p, axis=0, keepdims=True)
        acc_scr[...] = alpha * acc_scr[...] + jnp.dot(vt, p.astype(BF16), preferred_element_type=F32)
        m_scr[...] = m_new

    if mode == "dil":
        def step(t, carry):
            tile_update(i - t, t, False)
            return carry
        lax.fori_loop(0, i + 1, step, 0)
    else:
        tile_update(i, 0, True)

        def step(t, carry):
            tile_update(i - t, t, False)
            return carry
        lax.fori_loop(1, i + 1, step, 0)

    sub_o = lax.broadcasted_iota(jnp.int32, (LANES, TQ), 0)
    outs = [acc_scr[:, m * TQ:(m + 1) * TQ] / l_scr[:, m * TQ:(m + 1) * TQ] for m in range(nmaps)]
    if mode == "diff":
        lam = lam_ref[0:1, 0:1]
        ot = jnp.where(sub_o < HEAD_DIM, outs[0] - lam * outs[1], outs[2] - lam * outs[3])
    else:
        ot = jnp.where(sub_o < HEAD_DIM, outs[0], outs[1])
    o = ot.T
    if mode == "diff":
        lane_o = lax.broadcasted_iota(jnp.int32, (TQ, LANES), 1)
        sq = o * o
        ss0 = jnp.sum(jnp.where(lane_o < HEAD_DIM, sq, 0.0), axis=1, keepdims=True)
        ss1 = jnp.sum(jnp.where(lane_o >= HEAD_DIM, sq, 0.0), axis=1, keepdims=True)
        var = jnp.where(lane_o < HEAD_DIM, ss0, ss1) * (1.0 / HEAD_DIM)
        o = o * lax.rsqrt(var + NORM_EPS) * sg_ref[...] * (1.0 - lam_init)
    o_ref[0] = o.astype(BF16)


def _attn(k, qt, vt, *, mode, g0, ngroups, extra=(), lam_init=0.0):
    b, s, _ = k.shape
    nmaps = 4 if mode == "diff" else 2
    mcols = nmaps * TQ
    in_specs = [
        pl.BlockSpec((1, 1, LANES, TQ), lambda bi, g, i: (bi, i, g0 + g, 0)),
        pl.BlockSpec((1, s, LANES), lambda bi, g, i: (bi, 0, g0 + g)),
        pl.BlockSpec((1, s // TK, LANES, TK), lambda bi, g, i: (bi, 0, g0 + g, 0)),
    ] + [_resident((1, LANES)) for _ in extra]
    scratch = [pltpu.VMEM((LANES, mcols), BF16), pltpu.VMEM((1, mcols), F32), pltpu.VMEM((1, mcols), F32),
               pltpu.VMEM((LANES, mcols), F32)]
    if mode == "moba":
        scratch += [pltpu.VMEM((BF16_SUBLANES, LANES), F32), pltpu.VMEM((BF16_SUBLANES, mcols), F32)]
    if mode == "dil":
        scratch.append(pltpu.VMEM((N_BIAS, TK, TQ), F32))
    return pl.pallas_call(
        functools.partial(_attn_body, mode=mode, nmaps=nmaps, lam_init=lam_init),
        out_shape=jax.ShapeDtypeStruct((b, s, ngroups * LANES), BF16),
        grid=(b, ngroups, s // TQ),
        in_specs=in_specs,
        out_specs=pl.BlockSpec((1, TQ, LANES), lambda bi, g, i: (bi, i, g)),
        scratch_shapes=scratch,
        compiler_params=_params("arbitrary", "arbitrary", "arbitrary"),
        name="attn_" + mode,
    )(qt, k, vt, *extra)


def _memkv_body(m_ref, g_ref, w_ref, o_ref):
    h = _rms(m_ref[0], g_ref[...]).astype(BF16)
    o_ref[0] = jnp.dot(h, w_ref[...], preferred_element_type=F32).astype(BF16)


def _memkv(mem, g, w_kv):
    b, m, d = mem.shape
    n = w_kv.shape[1]
    return pl.pallas_call(
        _memkv_body,
        out_shape=jax.ShapeDtypeStruct((b, m, n), BF16),
        grid=(b,),
        in_specs=[pl.BlockSpec((1, m, d), lambda i: (i, 0, 0)), _resident((1, d)), _resident((d, n))],
        out_specs=pl.BlockSpec((1, m, n), lambda i: (i, 0, 0)),
        compiler_params=_params("parallel"),
        name="mem_kv",
    )(mem, g, w_kv)


def _post_body(x_ref, oa_ref, ob_ref, oc_ref, woa_ref, wob_ref, woc_ref,
               g_ref, wq_ref, kv_ref, wo_ref, o_ref):
    x = x_ref[...]
    x = x + jnp.dot(oa_ref[...], woa_ref[...], preferred_element_type=F32)
    x = x + jnp.dot(ob_ref[...], wob_ref[...], preferred_element_type=F32)
    x = x + jnp.dot(oc_ref[...], woc_ref[...], preferred_element_type=F32)

    h = _rms(x, g_ref[...]).astype(BF16)
    q = jnp.dot(h, wq_ref[...], preferred_element_type=F32) * (HEAD_DIM ** -0.5)
    tm = q.shape[0]
    lane = lax.broadcasted_iota(jnp.int32, (tm, LANES), 1)
    outs = []
    for gi in range(MEM_WIDTH // LANES):
        qg = q[:, gi * LANES:(gi + 1) * LANES]
        qm = jnp.concatenate([jnp.where(lane < HEAD_DIM, qg, 0.0),
                              jnp.where(lane >= HEAD_DIM, qg, 0.0)], axis=0).astype(BF16)
        kg = kv_ref[0, :, gi * LANES:(gi + 1) * LANES]
        vg = kv_ref[0, :, MEM_WIDTH + gi * LANES:MEM_WIDTH + (gi + 1) * LANES]
        s = lax.dot_general(qm, kg, (((1,), (1,)), ((), ())), preferred_element_type=F32)
        p = jnp.exp(s - jnp.max(s, axis=1, keepdims=True))
        l = jnp.sum(p, axis=1, keepdims=True)
        o = jnp.dot(p.astype(BF16), vg, preferred_element_type=F32) / l
        outs.append(jnp.where(lane < HEAD_DIM, o[0:tm], o[tm:2 * tm]).astype(BF16))
    o_mem = jnp.concatenate(outs, axis=1)
    o_ref[...] = x + jnp.dot(o_mem, wo_ref[...], preferred_element_type=F32)


def _post(x, oa, ob, oc, woa, wob, woc, g, wq, kv, wo, seq):
    t, d = x.shape
    tiles_per_seq = seq // TM
    tile = lambda w: pl.BlockSpec((TM, w), lambda i: (i, 0))
    m, n = kv.shape[1], kv.shape[2]
    return pl.pallas_call(
        _post_body,
        out_shape=jax.ShapeDtypeStruct((t, d), F32),
        grid=(t // TM,),
        in_specs=[tile(d), tile(oa.shape[1]), tile(ob.shape[1]), tile(oc.shape[1]),
                  _resident(woa.shape), _resident(wob.shape), _resident(woc.shape),
                  _resident((1, d)), _resident(wq.shape),
                  pl.BlockSpec((1, m, n), lambda i: (i // tiles_per_seq, 0, 0)),
                  _resident(wo.shape)],
        out_specs=tile(d),
        compiler_params=_params("parallel"),
        name="mix_out_mem",
    )(x, oa, ob, oc, woa, wob, woc, g, wq, kv, wo)


def _rotary_tables(positions, head_width):
    rot = head_width // ROPE_FRACTION
    inv_freq = ROPE_THETA ** (-jnp.arange(0, rot, 2, dtype=F32) / rot)
    ang = positions.astype(F32).reshape(-1, 1) * inv_freq
    cos, sin = jnp.cos(ang), jnp.sin(ang)
    t = ang.shape[0]
    rest = head_width - rot
    c_unit = jnp.concatenate([cos, cos, jnp.ones((t, rest), F32)], axis=-1)
    s_unit = jnp.concatenate([-sin, sin, jnp.zeros((t, rest), F32)], axis=-1)
    reps = LANES // head_width
    return jnp.tile(c_unit, (1, reps)), jnp.tile(s_unit, (1, reps))


def kernel(x, mem, positions, ffn1_norm, ffn1_w_gate, ffn1_w_up, ffn1_w_down, mix_norm, w_in, w_out,
           diff_lambda_q1, diff_lambda_k1, diff_lambda_q2, diff_lambda_k2, diff_subln,
           memq_norm, memkv_norm, mem_w_q, mem_w_kv, mem_w_o,
           ffn2_norm, ffn2_w_gate, ffn2_w_up, ffn2_w_down, final_norm):
    b, s, d = x.shape
    depth = w_in.shape[0]
    assert d == D_MODEL and s % TM == 0 and TM % TQ == 0 and TQ == TK == MOBA_BLOCK
    assert s <= DILATED_BRANCHES[-1][0]
    a_w, b_w, c_w = A_HEADS * HEAD_DIM, B_HEADS * HEAD_DIM, C_HEADS * HEAD_DIM

    tabs_n = _rotary_tables(positions, A_QK_DIM) + _rotary_tables(positions, HEAD_DIM)
    tabs_t = tuple(tab.T for tab in tabs_n)
    row = lambda v: v.reshape(1, -1).astype(F32)
    bf = lambda w: w.astype(BF16)

    bounds, c0 = {}, 0
    for name, w in (("qa", a_w), ("ka", a_w), ("va", a_w), ("qb", b_w), ("kb", b_w), ("vb", b_w),
                    ("qc", c_w), ("kc", c_w), ("vc", c_w)):
        bounds[name] = (c0, c0 + w)
        c0 += w
    pick = lambda w, names: jnp.concatenate([w[:, bounds[n][0]:bounds[n][1]] for n in names], axis=1)

    xt = x.reshape(b * s, d)
    for l in range(depth):
        lam_init = 0.8 - 0.6 * math.exp(-0.3 * l)
        lam = (jnp.exp(jnp.sum(diff_lambda_q1[l].astype(F32) * diff_lambda_k1[l].astype(F32)))
               - jnp.exp(jnp.sum(diff_lambda_q2[l].astype(F32) * diff_lambda_k2[l].astype(F32)))
               + lam_init)
        lam_row = jnp.full((1, LANES), lam, F32)
        subln_row = jnp.tile(row(diff_subln[l]), (1, LANES // HEAD_DIM))

        xt = _ffn(xt, row(ffn1_norm[l]), bf(ffn1_w_gate[l]), bf(ffn1_w_up[l]), bf(ffn1_w_down[l]))

        wl = bf(w_in[l])
        k, qt, vt = _proj(xt.reshape(b, s, d), row(mix_norm[l]), pick(wl, ("ka", "kb", "kc")),
                          pick(wl, ("qa", "qb", "qc")).T, pick(wl, ("va", "vb", "vc")).T, tabs_n, tabs_t)
        oa = _attn(k, qt, vt, mode="diff", g0=0, ngroups=A_GROUPS, extra=(lam_row, subln_row), lam_init=lam_init)
        ob = _attn(k, qt, vt, mode="dil", g0=A_GROUPS, ngroups=B_GROUPS)
        oc = _attn(k, qt, vt, mode="moba", g0=A_GROUPS + B_GROUPS, ngroups=C_GROUPS)

        kv = _memkv(mem, row(memkv_norm[l]), bf(mem_w_kv[l]))
        wo = bf(w_out[l])
        xt = _post(xt, oa.reshape(b * s, -1), ob.reshape(b * s, -1), oc.reshape(b * s, -1),
                   wo[:a_w], wo[a_w:a_w + b_w], wo[a_w + b_w:],
                   row(memq_norm[l]), bf(mem_w_q[l]), kv, bf(mem_w_o[l]), s)

        last = l == depth - 1
        xt = _ffn(xt, row(ffn2_norm[l]), bf(ffn2_w_gate[l]), bf(ffn2_w_up[l]), bf(ffn2_w_down[l]),
                  final_g=row(final_norm) if last else None)
    return xt.reshape(b, s, d)
```

```python
import functools
import math

import jax
import jax.numpy as jnp
from jax import lax
from jax.experimental import pallas as pl
from jax.experimental.pallas import tpu as pltpu

F32 = jnp.float32
BF16 = jnp.bfloat16

D_MODEL = 1024
HEAD_DIM = 64
A_HEADS, B_HEADS, C_HEADS = 4, 6, 6
A_QK_DIM = HEAD_DIM // 2
ROPE_THETA = 500000.0
ROPE_FRACTION = 4
DILATED_BRANCHES = ((128, 1), (512, 4), (2048, 16))
MOBA_BLOCK = 256
MOBA_TOP_K = 3
MEM_HEADS = 4
MEM_WIDTH = MEM_HEADS * HEAD_DIM
D_FF = 2816
NORM_EPS = 1e-6
NEG = -1e30
LOG2E = math.log2(math.e)

LANES = 128
BF16_SUBLANES = 16
VMEM_LIMIT = 56 * 1024 * 1024

TM = 512
FF_CHUNK = 512
PROJ_CHUNK = 512
TQ = 256
TK = 256
N_GROUPS = D_MODEL // LANES
A_GROUPS, B_GROUPS, C_GROUPS = 2, 3, 3
VT_ROWS = LANES + BF16_SUBLANES


def _group_kind(g):
    if g < A_GROUPS:
        return "A", A_QK_DIM ** -0.5 * LOG2E
    return "H", HEAD_DIM ** -0.5 * LOG2E


def _rms(x, g):
    return x * lax.rsqrt(jnp.mean(x * x, axis=-1, keepdims=True) + NORM_EPS) * g


def _resident(shape):
    nd = len(shape)
    return pl.BlockSpec(shape, lambda *_: (0,) * nd)


def _params(*sem):
    return pltpu.CompilerParams(dimension_semantics=sem, vmem_limit_bytes=VMEM_LIMIT)


def _ffn_body(*refs, final):
    if final:
        x_ref, g_ref, wg_ref, wu_ref, wd_ref, fg_ref, o_ref = refs
    else:
        x_ref, g_ref, wg_ref, wu_ref, wd_ref, o_ref = refs
    x = x_ref[...]
    h = _rms(x, g_ref[...]).astype(BF16)
    acc = jnp.zeros(x.shape, F32)
    for c0 in range(0, D_FF, FF_CHUNK):
        c1 = min(c0 + FF_CHUNK, D_FF)
        a = jnp.dot(h, wg_ref[:, c0:c1], preferred_element_type=F32)
        u = jnp.dot(h, wu_ref[:, c0:c1], preferred_element_type=F32)
        t = (a / (1.0 + jnp.exp(-a))) * u
        acc = acc + jnp.dot(t.astype(BF16), wd_ref[c0:c1, :], preferred_element_type=F32)
    y = x + 0.5 * acc
    if final:
        y = _rms(y, fg_ref[...])
    o_ref[...] = y


def _ffn(x, g, wg, wu, wd, final_g=None):
    t, d = x.shape
    final = final_g is not None
    in_specs = [
        pl.BlockSpec((TM, d), lambda i: (i, 0)),
        _resident((1, d)),
        _resident((d, D_FF)),
        _resident((d, D_FF)),
        _resident((D_FF, d)),
    ]
    args = [x, g, wg, wu, wd]
    if final:
        in_specs.append(_resident((1, d)))
        args.append(final_g)
    return pl.pallas_call(
        functools.partial(_ffn_body, final=final),
        out_shape=jax.ShapeDtypeStruct((t, d), F32),
        grid=(t // TM,),
        in_specs=in_specs,
        out_specs=pl.BlockSpec((TM, d), lambda i: (i, 0)),
        compiler_params=_params("parallel"),
        name="ffn_final" if final else "ffn",
    )(*args)


def _rotate_half(blk, first, half, axis):
    n = blk.shape[axis]
    return jnp.where(first, pltpu.roll(blk, n - half, axis), pltpu.roll(blk, half, axis))


def _proj_body(x_ref, g_ref, wk_ref, wqt_ref, wvt_ref, ca_ref, sa_ref, ch_ref, sh_ref,
               cat_ref, sat_ref, cht_ref, sht_ref, k_ref, qt_ref, vt_ref):
    h = _rms(x_ref[0], g_ref[...]).astype(BF16)
    tm = h.shape[0]
    half = {"A": A_QK_DIM // ROPE_FRACTION // 2, "H": HEAD_DIM // ROPE_FRACTION // 2}
    width = {"A": A_QK_DIM, "H": HEAD_DIM}
    nt = (((1,), (1,)), ((), ()))
    per_chunk = PROJ_CHUNK // LANES

    lane = lax.broadcasted_iota(jnp.int32, (tm, LANES), 1)
    first_n = {k: (lane & (width[k] - 1)) < half[k] for k in ("A", "H")}
    tab_n = {"A": (ca_ref, sa_ref), "H": (ch_ref, sh_ref)}
    for c in range(N_GROUPS // per_chunk):
        y = jnp.dot(h, wk_ref[:, c * PROJ_CHUNK:(c + 1) * PROJ_CHUNK], preferred_element_type=F32)
        for j in range(per_chunk):
            g = c * per_chunk + j
            kind, _ = _group_kind(g)
            blk = y[:, j * LANES:(j + 1) * LANES]
            cos_ref, sin_ref = tab_n[kind]
            blk = blk * cos_ref[...] + _rotate_half(blk, first_n[kind], half[kind], 1) * sin_ref[...]
            k_ref[0, :, g * LANES:(g + 1) * LANES] = blk.astype(BF16)

    sub = lax.broadcasted_iota(jnp.int32, (LANES, tm), 0)
    first_t = {k: (sub & (width[k] - 1)) < half[k] for k in ("A", "H")}
    tab_t = {"A": (cat_ref, sat_ref), "H": (cht_ref, sht_ref)}
    for c in range(N_GROUPS // per_chunk):
        y = lax.dot_general(wqt_ref[c * PROJ_CHUNK:(c + 1) * PROJ_CHUNK, :], h, nt,
                            preferred_element_type=F32)
        for j in range(per_chunk):
            g = c * per_chunk + j
            kind, scale = _group_kind(g)
            blk = y[j * LANES:(j + 1) * LANES, :]
            cos_ref, sin_ref = tab_t[kind]
            blk = blk * cos_ref[...] + _rotate_half(blk, first_t[kind], half[kind], 0) * sin_ref[...]
            qt_ref[0, g * LANES:(g + 1) * LANES, :] = (blk * scale).astype(BF16)

    for c in range(N_GROUPS // per_chunk):
        y = lax.dot_general(wvt_ref[c * PROJ_CHUNK:(c + 1) * PROJ_CHUNK, :], h, nt, preferred_element_type=F32)
        vt_ref[0, c * PROJ_CHUNK:(c + 1) * PROJ_CHUNK, :] = y.astype(BF16)


def _proj(x, g, wk, wqt, wvt, tabs_n, tabs_t):
    b, s, d = x.shape
    per_seq = s // TM
    tab_n = pl.BlockSpec((TM, LANES), lambda bi, i: (bi * per_seq + i, 0))
    tab_t = pl.BlockSpec((LANES, TM), lambda bi, i: (0, bi * per_seq + i))
    slab = pl.BlockSpec((1, d, TM), lambda bi, i: (bi, 0, i))
    return pl.pallas_call(
        _proj_body,
        out_shape=(jax.ShapeDtypeStruct((b, s, d), BF16),
                   jax.ShapeDtypeStruct((b, d, s), BF16),
                   jax.ShapeDtypeStruct((b, d, s), BF16)),
        grid=(b, per_seq),
        in_specs=[pl.BlockSpec((1, TM, d), lambda bi, i: (bi, i, 0)), _resident((1, d)),
                  _resident((d, d)), _resident((d, d)), _resident((d, d)),
                  tab_n, tab_n, tab_n, tab_n, tab_t, tab_t, tab_t, tab_t],
        out_specs=(pl.BlockSpec((1, TM, d), lambda bi, i: (bi, i, 0)), slab, slab),
        compiler_params=_params("parallel", "parallel"),
        name="mix_proj",
    )(x, g, wk, wqt, wvt, *tabs_n, *tabs_t)


def _dilated_bias(offset):
    r = lax.broadcasted_iota(jnp.int32, (TK, TQ), 0)
    c = lax.broadcasted_iota(jnp.int32, (TK, TQ), 1)
    delta = c - r + offset * TK
    cnt = jnp.zeros((TK, TQ), jnp.int32)
    for window, dilation in DILATED_BRANCHES:
        hit = (delta <= window) & ((delta & (dilation - 1)) == 0)
        cnt = cnt + jnp.where(hit, 1, 0)
    cnt = jnp.where(delta >= 0, cnt, 0)
    return jnp.where(cnt == 3, math.log2(3.0),
                     jnp.where(cnt == 2, 1.0, jnp.where(cnt == 1, 0.0, NEG))).astype(F32)


N_BIAS = 4


def _attn_body(*refs, mode, nmaps, lam_init):
    if mode == "diff":
        qt_ref, k_ref, vt_ref, lam_ref, sg_ref, o_ref, qm_scr = refs
    elif mode == "moba":
        qt_ref, k_ref, vt_ref, o_ref, qm_scr, km_scr, sel_scr = refs
    else:
        qt_ref, k_ref, vt_ref, o_ref, qm_scr, bias_scr = refs
    i = pl.program_id(2)
    nq = k_ref.shape[1] // TQ
    mcols = nmaps * TQ
    width = LANES // nmaps

    sub_q = lax.broadcasted_iota(jnp.int32, (LANES, TQ), 0)
    q32 = qt_ref[0].astype(F32)
    for m in range(nmaps):
        keep = (sub_q >= m * width) & (sub_q < (m + 1) * width)
        qm_scr[:, m * TQ:(m + 1) * TQ] = jnp.where(keep, q32, 0.0).astype(BF16)

    if mode == "dil":
        @pl.when((pl.program_id(0) == 0) & (pl.program_id(1) == 0) & (i == 0))
        def _():
            for off in range(N_BIAS):
                bias_scr[off] = _dilated_bias(off)

    if mode == "moba":
        nblk = k_ref.shape[1] // MOBA_BLOCK

        @pl.when(i == 0)
        def _():
            km_scr[...] = jnp.zeros(km_scr.shape, F32)
            for j in range(nblk):
                kb = k_ref[0, j * MOBA_BLOCK:(j + 1) * MOBA_BLOCK, :].astype(F32)
                km_scr[j:j + 1, :] = jnp.mean(kb, axis=0, keepdims=True)

        km = km_scr[...]
        km_hi = km.astype(BF16)
        km_lo = (km - km_hi.astype(F32)).astype(BF16)
        qm = qm_scr[...]
        gate = (jnp.dot(km_hi, qm, preferred_element_type=F32)
                + jnp.dot(km_lo, qm, preferred_element_type=F32))
        blk_id = lax.broadcasted_iota(jnp.int32, gate.shape, 0)
        rank = jnp.zeros(gate.shape, F32)
        for jp in range(nblk):
            rowj = gate[jp:jp + 1, :]
            beats = (rowj > gate) | ((rowj == gate) & (blk_id > jp))
            rank = rank + jnp.where(beats, 1.0, 0.0) * jnp.where(jp < i, 1.0, 0.0)
        sel_scr[...] = jnp.where((rank < MOBA_TOP_K - 0.5) & (blk_id < i), 1.0, 0.0)

    def finish(acc):
        sub_o = lax.broadcasted_iota(jnp.int32, (LANES, TQ), 0)
        outs = [acc[0:LANES, m * TQ:(m + 1) * TQ] / acc[LANES:LANES + 1, m * TQ:(m + 1) * TQ]
                for m in range(nmaps)]
        if mode == "diff":
            lam = lam_ref[0:1, 0:1]
            ot = jnp.where(sub_o < HEAD_DIM, outs[0] - lam * outs[1], outs[2] - lam * outs[3])
        else:
            ot = jnp.where(sub_o < HEAD_DIM, outs[0], outs[1])
        o = ot.T
        if mode == "diff":
            lane_o = lax.broadcasted_iota(jnp.int32, (TQ, LANES), 1)
            sq = o * o
            ss0 = jnp.sum(jnp.where(lane_o < HEAD_DIM, sq, 0.0), axis=1, keepdims=True)
            ss1 = jnp.sum(jnp.where(lane_o >= HEAD_DIM, sq, 0.0), axis=1, keepdims=True)
            var = jnp.where(lane_o < HEAD_DIM, ss0, ss1) * (1.0 / HEAD_DIM)
            o = o * lax.rsqrt(var + NORM_EPS) * sg_ref[...] * (1.0 - lam_init)
        o_ref[0] = o.astype(BF16)

    ones_rows = jnp.ones((BF16_SUBLANES, TK), BF16)

    def scores(c, b):
        s = jnp.dot(k_ref[0, b * TK:(b + 1) * TK, :], qm_scr[...], preferred_element_type=F32)
        if mode == "dil":
            return s + jnp.concatenate([bias_scr[min(c - b, N_BIAS - 1)]] * nmaps, axis=1)
        if b == c:
            kpos = lax.broadcasted_iota(jnp.int32, (TK, mcols), 0)
            qpos = lax.broadcasted_iota(jnp.int32, (TK, mcols), 1) & (TQ - 1)
            return jnp.where(kpos <= qpos, s, NEG)
        return s

    def query_tile(c):
        order = [c] + list(range(c))
        s_next = scores(c, order[0])
        m = acc = None
        for idx, b in enumerate(order):
            s = s_next
            if idx + 1 < len(order):
                s_next = scores(c, order[idx + 1])
            top = jnp.max(s, axis=0, keepdims=True)
            gated = mode == "moba" and b < c
            if gated:
                chosen = sel_scr[b:b + 1, :] > 0.5
                top = jnp.where(chosen, top, NEG)
            m_new = top if m is None else jnp.maximum(m, top)
            sub = jnp.where(chosen, m_new, -NEG) if gated else m_new
            p = jnp.exp2(s - sub).astype(BF16)
            vt = jnp.concatenate([vt_ref[0, :, b * TK:(b + 1) * TK], ones_rows], axis=0)
            pv = jnp.dot(vt, p, preferred_element_type=F32)
            acc = pv if acc is None else jnp.exp2(m - m_new) * acc + pv
            m = m_new
        finish(acc)

    for c in range(nq):
        pl.when(i == c)(functools.partial(query_tile, c))


def _attn(k, qt, vt, *, mode, g0, ngroups, extra=(), lam_init=0.0):
    b, s, _ = k.shape
    nmaps = 4 if mode == "diff" else 2
    mcols = nmaps * TQ
    in_specs = [
        pl.BlockSpec((1, LANES, TQ), lambda bi, g, i: (bi, g0 + g, i)),
        pl.BlockSpec((1, s, LANES), lambda bi, g, i: (bi, 0, g0 + g)),
        pl.BlockSpec((1, LANES, s), lambda bi, g, i: (bi, g0 + g, 0)),
    ] + [_resident((1, LANES)) for _ in extra]
    scratch = [pltpu.VMEM((LANES, mcols), BF16)]
    if mode == "moba":
        scratch += [pltpu.VMEM((BF16_SUBLANES, LANES), F32), pltpu.VMEM((BF16_SUBLANES, mcols), F32)]
    if mode == "dil":
        scratch.append(pltpu.VMEM((N_BIAS, TK, TQ), F32))
    return pl.pallas_call(
        functools.partial(_attn_body, mode=mode, nmaps=nmaps, lam_init=lam_init),
        out_shape=jax.ShapeDtypeStruct((b, s, ngroups * LANES), BF16),
        grid=(b, ngroups, s // TQ),
        in_specs=in_specs,
        out_specs=pl.BlockSpec((1, TQ, LANES), lambda bi, g, i: (bi, i, g)),
        scratch_shapes=scratch,
        compiler_params=_params("arbitrary", "arbitrary", "arbitrary"),
        name="attn_" + mode,
    )(qt, k, vt, *extra)


def _memkv_body(m_ref, g_ref, w_ref, o_ref):
    h = _rms(m_ref[0], g_ref[...]).astype(BF16)
    o_ref[0] = jnp.dot(h, w_ref[...], preferred_element_type=F32).astype(BF16)


def _memkv(mem, g, w_kv):
    b, m, d = mem.shape
    n = w_kv.shape[1]
    return pl.pallas_call(
        _memkv_body,
        out_shape=jax.ShapeDtypeStruct((b, m, n), BF16),
        grid=(b,),
        in_specs=[pl.BlockSpec((1, m, d), lambda i: (i, 0, 0)), _resident((1, d)), _resident((d, n))],
        out_specs=pl.BlockSpec((1, m, n), lambda i: (i, 0, 0)),
        compiler_params=_params("parallel"),
        name="mem_kv",
    )(mem, g, w_kv)


def _post_body(x_ref, oa_ref, ob_ref, oc_ref, woa_ref, wob_ref, woc_ref,
               g_ref, wq_ref, kv_ref, wo_ref, o_ref):
    x = x_ref[...]
    x = x + jnp.dot(oa_ref[...], woa_ref[...], preferred_element_type=F32)
    x = x + jnp.dot(ob_ref[...], wob_ref[...], preferred_element_type=F32)
    x = x + jnp.dot(oc_ref[...], woc_ref[...], preferred_element_type=F32)

    h = _rms(x, g_ref[...]).astype(BF16)
    q = jnp.dot(h, wq_ref[...], preferred_element_type=F32) * (HEAD_DIM ** -0.5)
    tm = q.shape[0]
    lane = lax.broadcasted_iota(jnp.int32, (tm, LANES), 1)
    outs = []
    for gi in range(MEM_WIDTH // LANES):
        qg = q[:, gi * LANES:(gi + 1) * LANES]
        qm = jnp.concatenate([jnp.where(lane < HEAD_DIM, qg, 0.0),
                              jnp.where(lane >= HEAD_DIM, qg, 0.0)], axis=0).astype(BF16)
        kg = kv_ref[0, :, gi * LANES:(gi + 1) * LANES]
        vg = kv_ref[0, :, MEM_WIDTH + gi * LANES:MEM_WIDTH + (gi + 1) * LANES]
        s = lax.dot_general(qm, kg, (((1,), (1,)), ((), ())), preferred_element_type=F32)
        p = jnp.exp(s - jnp.max(s, axis=1, keepdims=True))
        l = jnp.sum(p, axis=1, keepdims=True)
        o = jnp.dot(p.astype(BF16), vg, preferred_element_type=F32) / l
        outs.append(jnp.where(lane < HEAD_DIM, o[0:tm], o[tm:2 * tm]).astype(BF16))
    o_mem = jnp.concatenate(outs, axis=1)
    o_ref[...] = x + jnp.dot(o_mem, wo_ref[...], preferred_element_type=F32)


def _post(x, oa, ob, oc, woa, wob, woc, g, wq, kv, wo, seq):
    t, d = x.shape
    tiles_per_seq = seq // TM
    tile = lambda w: pl.BlockSpec((TM, w), lambda i: (i, 0))
    m, n = kv.shape[1], kv.shape[2]
    return pl.pallas_call(
        _post_body,
        out_shape=jax.ShapeDtypeStruct((t, d), F32),
        grid=(t // TM,),
        in_specs=[tile(d), tile(oa.shape[1]), tile(ob.shape[1]), tile(oc.shape[1]),
                  _resident(woa.shape), _resident(wob.shape), _resident(woc.shape),
                  _resident((1, d)), _resident(wq.shape),
                  pl.BlockSpec((1, m, n), lambda i: (i // tiles_per_seq, 0, 0)),
                  _resident(wo.shape)],
        out_specs=tile(d),
        compiler_params=_params("parallel"),
        name="mix_out_mem",
    )(x, oa, ob, oc, woa, wob, woc, g, wq, kv, wo)


def _rotary_tables(positions, head_width):
    rot = head_width // ROPE_FRACTION
    inv_freq = ROPE_THETA ** (-jnp.arange(0, rot, 2, dtype=F32) / rot)
    ang = positions.astype(F32).reshape(-1, 1) * inv_freq
    cos, sin = jnp.cos(ang), jnp.sin(ang)
    t = ang.shape[0]
    rest = head_width - rot
    c_unit = jnp.concatenate([cos, cos, jnp.ones((t, rest), F32)], axis=-1)
    s_unit = jnp.concatenate([-sin, sin, jnp.zeros((t, rest), F32)], axis=-1)
    reps = LANES // head_width
    return jnp.tile(c_unit, (1, reps)), jnp.tile(s_unit, (1, reps))


def kernel(x, mem, positions, ffn1_norm, ffn1_w_gate, ffn1_w_up, ffn1_w_down, mix_norm, w_in, w_out,
           diff_lambda_q1, diff_lambda_k1, diff_lambda_q2, diff_lambda_k2, diff_subln,
           memq_norm, memkv_norm, mem_w_q, mem_w_kv, mem_w_o,
           ffn2_norm, ffn2_w_gate, ffn2_w_up, ffn2_w_down, final_norm):
    b, s, d = x.shape
    depth = w_in.shape[0]
    assert d == D_MODEL and s % TM == 0 and TM % TQ == 0 and TQ == TK == MOBA_BLOCK
    assert s <= DILATED_BRANCHES[-1][0]
    a_w, b_w, c_w = A_HEADS * HEAD_DIM, B_HEADS * HEAD_DIM, C_HEADS * HEAD_DIM

    tabs_n = _rotary_tables(positions, A_QK_DIM) + _rotary_tables(positions, HEAD_DIM)
    tabs_t = tuple(tab.T for tab in tabs_n)
    row = lambda v: v.reshape(1, -1).astype(F32)
    bf = lambda w: w.astype(BF16)

    bounds, c0 = {}, 0
    for name, w in (("qa", a_w), ("ka", a_w), ("va", a_w), ("qb", b_w), ("kb", b_w), ("vb", b_w),
                    ("qc", c_w), ("kc", c_w), ("vc", c_w)):
        bounds[name] = (c0, c0 + w)
        c0 += w
    pick = lambda w, names: jnp.concatenate([w[:, bounds[n][0]:bounds[n][1]] for n in names], axis=1)

    xt = x.reshape(b * s, d)
    for l in range(depth):
        lam_init = 0.8 - 0.6 * math.exp(-0.3 * l)
        lam = (jnp.exp(jnp.sum(diff_lambda_q1[l].astype(F32) * diff_lambda_k1[l].astype(F32)))
               - jnp.exp(jnp.sum(diff_lambda_q2[l].astype(F32) * diff_lambda_k2[l].astype(F32)))
               + lam_init)
        lam_row = jnp.full((1, LANES), lam, F32)
        subln_row = jnp.tile(row(diff_subln[l]), (1, LANES // HEAD_DIM))

        xt = _ffn(xt, row(ffn1_norm[l]), bf(ffn1_w_gate[l]), bf(ffn1_w_up[l]), bf(ffn1_w_down[l]))

        wl = bf(w_in[l])
        k, qt, vt = _proj(xt.reshape(b, s, d), row(mix_norm[l]), pick(wl, ("ka", "kb", "kc")),
                          pick(wl, ("qa", "qb", "qc")).T, pick(wl, ("va", "vb", "vc")).T, tabs_n, tabs_t)
        oa = _attn(k, qt, vt, mode="diff", g0=0, ngroups=A_GROUPS, extra=(lam_row, subln_row), lam_init=lam_init)
        ob = _attn(k, qt, vt, mode="dil", g0=A_GROUPS, ngroups=B_GROUPS)
        oc = _attn(k, qt, vt, mode="moba", g0=A_GROUPS + B_GROUPS, ngroups=C_GROUPS)

        kv = _memkv(mem, row(memkv_norm[l]), bf(mem_w_kv[l]))
        wo = bf(w_out[l])
        xt = _post(xt, oa.reshape(b * s, -1), ob.reshape(b * s, -1), oc.reshape(b * s, -1),
                   wo[:a_w], wo[a_w:a_w + b_w], wo[a_w + b_w:],
                   row(memq_norm[l]), bf(mem_w_q[l]), kv, bf(mem_w_o[l]), s)

        last = l == depth - 1
        xt = _ffn(xt, row(ffn2_norm[l]), bf(ffn2_w_gate[l]), bf(ffn2_w_up[l]), bf(ffn2_w_down[l]),
                  final_g=row(final_norm) if last else None)
    return xt.reshape(b, s, d)
```

```python
import functools
import math

import jax
import jax.numpy as jnp
from jax import lax
from jax.experimental import pallas as pl
from jax.experimental.pallas import tpu as pltpu

F32 = jnp.float32
BF16 = jnp.bfloat16

D_MODEL = 1024
HEAD_DIM = 64
A_HEADS, B_HEADS, C_HEADS = 4, 6, 6
A_QK_DIM = HEAD_DIM // 2
ROPE_THETA = 500000.0
ROPE_FRACTION = 4
DILATED_BRANCHES = ((128, 1), (512, 4), (2048, 16))
MOBA_BLOCK = 256
MOBA_TOP_K = 3
MEM_HEADS = 4
MEM_WIDTH = MEM_HEADS * HEAD_DIM
D_FF = 2816
NORM_EPS = 1e-6
NEG = -1e30
LOG2E = math.log2(math.e)

LANES = 128
BF16_SUBLANES = 16
VMEM_LIMIT = 56 * 1024 * 1024

TM = 512
FF_CHUNK = 512
PROJ_CHUNK = 512
TQ = 256
TK = 256
N_GROUPS = D_MODEL // LANES
A_GROUPS, B_GROUPS, C_GROUPS = 2, 3, 3
VT_ROWS = LANES + BF16_SUBLANES


def _group_kind(g):
    if g < A_GROUPS:
        return "A", A_QK_DIM ** -0.5 * LOG2E
    return "H", HEAD_DIM ** -0.5 * LOG2E


def _rms(x, g):
    return x * lax.rsqrt(jnp.mean(x * x, axis=-1, keepdims=True) + NORM_EPS) * g


def _resident(shape):
    nd = len(shape)
    return pl.BlockSpec(shape, lambda *_: (0,) * nd)


def _params(*sem):
    return pltpu.CompilerParams(dimension_semantics=sem, vmem_limit_bytes=VMEM_LIMIT)


def _ffn_body(*refs, final):
    if final:
        x_ref, g_ref, wg_ref, wu_ref, wd_ref, fg_ref, o_ref = refs
    else:
        x_ref, g_ref, wg_ref, wu_ref, wd_ref, o_ref = refs
    x = x_ref[...]
    h = _rms(x, g_ref[...]).astype(BF16)
    acc = jnp.zeros(x.shape, F32)
    for c0 in range(0, D_FF, FF_CHUNK):
        c1 = min(c0 + FF_CHUNK, D_FF)
        a = jnp.dot(h, wg_ref[:, c0:c1], preferred_element_type=F32)
        u = jnp.dot(h, wu_ref[:, c0:c1], preferred_element_type=F32)
        t = (a / (1.0 + jnp.exp(-a))) * u
        acc = acc + jnp.dot(t.astype(BF16), wd_ref[c0:c1, :], preferred_element_type=F32)
    y = x + 0.5 * acc
    if final:
        y = _rms(y, fg_ref[...])
    o_ref[...] = y


def _ffn(x, g, wg, wu, wd, final_g=None):
    t, d = x.shape
    final = final_g is not None
    in_specs = [
        pl.BlockSpec((TM, d), lambda i: (i, 0)),
        _resident((1, d)),
        _resident((d, D_FF)),
        _resident((d, D_FF)),
        _resident((D_FF, d)),
    ]
    args = [x, g, wg, wu, wd]
    if final:
        in_specs.append(_resident((1, d)))
        args.append(final_g)
    return pl.pallas_call(
        functools.partial(_ffn_body, final=final),
        out_shape=jax.ShapeDtypeStruct((t, d), F32),
        grid=(t // TM,),
        in_specs=in_specs,
        out_specs=pl.BlockSpec((TM, d), lambda i: (i, 0)),
        compiler_params=_params("parallel"),
        name="ffn_final" if final else "ffn",
    )(*args)


def _rotate_half(blk, first, half, axis):
    n = blk.shape[axis]
    return jnp.where(first, pltpu.roll(blk, n - half, axis), pltpu.roll(blk, half, axis))


def _proj_body(x_ref, g_ref, wk_ref, wqt_ref, wvt_ref, ca_ref, sa_ref, ch_ref, sh_ref,
               cat_ref, sat_ref, cht_ref, sht_ref, k_ref, qt_ref, vt_ref):
    h = _rms(x_ref[0], g_ref[...]).astype(BF16)
    tm = h.shape[0]
    half = {"A": A_QK_DIM // ROPE_FRACTION // 2, "H": HEAD_DIM // ROPE_FRACTION // 2}
    width = {"A": A_QK_DIM, "H": HEAD_DIM}
    nt = (((1,), (1,)), ((), ()))
    per_chunk = PROJ_CHUNK // LANES

    lane = lax.broadcasted_iota(jnp.int32, (tm, LANES), 1)
    first_n = {k: (lane & (width[k] - 1)) < half[k] for k in ("A", "H")}
    tab_n = {"A": (ca_ref, sa_ref), "H": (ch_ref, sh_ref)}
    for c in range(N_GROUPS // per_chunk):
        y = jnp.dot(h, wk_ref[:, c * PROJ_CHUNK:(c + 1) * PROJ_CHUNK], preferred_element_type=F32)
        for j in range(per_chunk):
            g = c * per_chunk + j
            kind, _ = _group_kind(g)
            blk = y[:, j * LANES:(j + 1) * LANES]
            cos_ref, sin_ref = tab_n[kind]
            blk = blk * cos_ref[...] + _rotate_half(blk, first_n[kind], half[kind], 1) * sin_ref[...]
            k_ref[0, :, g * LANES:(g + 1) * LANES] = blk.astype(BF16)

    sub = lax.broadcasted_iota(jnp.int32, (LANES, tm), 0)
    first_t = {k: (sub & (width[k] - 1)) < half[k] for k in ("A", "H")}
    tab_t = {"A": (cat_ref, sat_ref), "H": (cht_ref, sht_ref)}
    for c in range(N_GROUPS // per_chunk):
        y = lax.dot_general(wqt_ref[c * PROJ_CHUNK:(c + 1) * PROJ_CHUNK, :], h, nt,
                            preferred_element_type=F32)
        for j in range(per_chunk):
            g = c * per_chunk + j
            kind, scale = _group_kind(g)
            blk = y[j * LANES:(j + 1) * LANES, :]
            cos_ref, sin_ref = tab_t[kind]
            blk = blk * cos_ref[...] + _rotate_half(blk, first_t[kind], half[kind], 0) * sin_ref[...]
            qt_ref[0, g * LANES:(g + 1) * LANES, :] = (blk * scale).astype(BF16)

    for c in range(N_GROUPS // per_chunk):
        y = lax.dot_general(wvt_ref[c * PROJ_CHUNK:(c + 1) * PROJ_CHUNK, :], h, nt, preferred_element_type=F32)
        vt_ref[0, c * PROJ_CHUNK:(c + 1) * PROJ_CHUNK, :] = y.astype(BF16)


def _proj(x, g, wk, wqt, wvt, tabs_n, tabs_t):
    b, s, d = x.shape
    per_seq = s // TM
    tab_n = pl.BlockSpec((TM, LANES), lambda bi, i: (bi * per_seq + i, 0))
    tab_t = pl.BlockSpec((LANES, TM), lambda bi, i: (0, bi * per_seq + i))
    slab = pl.BlockSpec((1, d, TM), lambda bi, i: (bi, 0, i))
    return pl.pallas_call(
        _proj_body,
        out_shape=(jax.ShapeDtypeStruct((b, s, d), BF16),
                   jax.ShapeDtypeStruct((b, d, s), BF16),
                   jax.ShapeDtypeStruct((b, d, s), BF16)),
        grid=(b, per_seq),
        in_specs=[pl.BlockSpec((1, TM, d), lambda bi, i: (bi, i, 0)), _resident((1, d)),
                  _resident((d, d)), _resident((d, d)), _resident((d, d)),
                  tab_n, tab_n, tab_n, tab_n, tab_t, tab_t, tab_t, tab_t],
        out_specs=(pl.BlockSpec((1, TM, d), lambda bi, i: (bi, i, 0)), slab, slab),
        compiler_params=_params("parallel", "parallel"),
        name="mix_proj",
    )(x, g, wk, wqt, wvt, *tabs_n, *tabs_t)


def _dilated_bias(offset):
    r = lax.broadcasted_iota(jnp.int32, (TK, TQ), 0)
    c = lax.broadcasted_iota(jnp.int32, (TK, TQ), 1)
    delta = c - r + offset * TK
    cnt = jnp.zeros((TK, TQ), jnp.int32)
    for window, dilation in DILATED_BRANCHES:
        hit = (delta <= window) & ((delta & (dilation - 1)) == 0)
        cnt = cnt + jnp.where(hit, 1, 0)
    cnt = jnp.where(delta >= 0, cnt, 0)
    return jnp.where(cnt == 3, math.log2(3.0),
                     jnp.where(cnt == 2, 1.0, jnp.where(cnt == 1, 0.0, NEG))).astype(F32)


N_BIAS = 4


def _attn_body(*refs, mode, nmaps, lam_init):
    if mode == "diff":
        qt_ref, k_ref, vt_ref, lam_ref, sg_ref, o_ref, qm_scr = refs
    elif mode == "moba":
        qt_ref, k_ref, vt_ref, o_ref, qm_scr, km_scr, sel_scr = refs
    else:
        qt_ref, k_ref, vt_ref, o_ref, qm_scr, bias_scr = refs
    nq = k_ref.shape[1] // TQ
    mcols = nmaps * TQ
    width = LANES // nmaps

    if mode == "dil":
        @pl.when((pl.program_id(0) == 0) & (pl.program_id(1) == 0))
        def _():
            for off in range(N_BIAS):
                bias_scr[off] = _dilated_bias(off)

    if mode == "moba":
        km_scr[...] = jnp.zeros(km_scr.shape, F32)
        for j in range(nq):
            kb = k_ref[0, j * MOBA_BLOCK:(j + 1) * MOBA_BLOCK, :].astype(F32)
            km_scr[j:j + 1, :] = jnp.mean(kb, axis=0, keepdims=True)
        km = km_scr[...]
        km_hi = km.astype(BF16)
        km_lo = (km - km_hi.astype(F32)).astype(BF16)

    def prepare(c):
        sub_q = lax.broadcasted_iota(jnp.int32, (LANES, TQ), 0)
        q32 = qt_ref[0, :, c * TQ:(c + 1) * TQ].astype(F32)
        for m in range(nmaps):
            keep = (sub_q >= m * width) & (sub_q < (m + 1) * width)
            qm_scr[c, :, m * TQ:(m + 1) * TQ] = jnp.where(keep, q32, 0.0).astype(BF16)
        if mode == "moba" and c > 0:
            qm = qm_scr[c]
            gate = (jnp.dot(km_hi, qm, preferred_element_type=F32)
                    + jnp.dot(km_lo, qm, preferred_element_type=F32))
            blk_id = lax.broadcasted_iota(jnp.int32, gate.shape, 0)
            rank = jnp.zeros(gate.shape, F32)
            for jp in range(c):
                rowj = gate[jp:jp + 1, :]
                beats = (rowj > gate) | ((rowj == gate) & (blk_id > jp))
                rank = rank + jnp.where(beats, 1.0, 0.0)
            sel_scr[c] = jnp.where((rank < MOBA_TOP_K - 0.5) & (blk_id < c), 1.0, 0.0)

    def finish(c, acc):
        sub_o = lax.broadcasted_iota(jnp.int32, (LANES, TQ), 0)
        outs = [acc[0:LANES, m * TQ:(m + 1) * TQ] / acc[LANES:LANES + 1, m * TQ:(m + 1) * TQ]
                for m in range(nmaps)]
        if mode == "diff":
            lam = lam_ref[0:1, 0:1]
            ot = jnp.where(sub_o < HEAD_DIM, outs[0] - lam * outs[1], outs[2] - lam * outs[3])
        else:
            ot = jnp.where(sub_o < HEAD_DIM, outs[0], outs[1])
        o = ot.T
        if mode == "diff":
            lane_o = lax.broadcasted_iota(jnp.int32, (TQ, LANES), 1)
            sq = o * o
            ss0 = jnp.sum(jnp.where(lane_o < HEAD_DIM, sq, 0.0), axis=1, keepdims=True)
            ss1 = jnp.sum(jnp.where(lane_o >= HEAD_DIM, sq, 0.0), axis=1, keepdims=True)
            var = jnp.where(lane_o < HEAD_DIM, ss0, ss1) * (1.0 / HEAD_DIM)
            o = o * lax.rsqrt(var + NORM_EPS) * sg_ref[...] * (1.0 - lam_init)
        o_ref[0, c * TQ:(c + 1) * TQ, :] = o.astype(BF16)

    ones_rows = jnp.ones((BF16_SUBLANES, TK), BF16)

    def scores(c, b):
        s = jnp.dot(k_ref[0, b * TK:(b + 1) * TK, :], qm_scr[c], preferred_element_type=F32)
        if mode == "dil":
            return s + jnp.concatenate([bias_scr[min(c - b, N_BIAS - 1)]] * nmaps, axis=1)
        if b == c:
            kpos = lax.broadcasted_iota(jnp.int32, (TK, mcols), 0)
            qpos = lax.broadcasted_iota(jnp.int32, (TK, mcols), 1) & (TQ - 1)
            return jnp.where(kpos <= qpos, s, NEG)
        return s

    def query_tile(c):
        order = [c] + list(range(c))
        s_next = scores(c, order[0])
        m = acc = None
        for idx, b in enumerate(order):
            s = s_next
            if idx + 1 < len(order):
                s_next = scores(c, order[idx + 1])
            top = jnp.max(s, axis=0, keepdims=True)
            gated = mode == "moba" and b < c
            if gated:
                chosen = sel_scr[c, b:b + 1, :] > 0.5
                top = jnp.where(chosen, top, NEG)
            m_new = top if m is None else jnp.maximum(m, top)
            sub = jnp.where(chosen, m_new, -NEG) if gated else m_new
            p = jnp.exp2(s - sub).astype(BF16)
            vt = jnp.concatenate([vt_ref[0, :, b * TK:(b + 1) * TK], ones_rows], axis=0)
            pv = jnp.dot(vt, p, preferred_element_type=F32)
            acc = pv if acc is None else jnp.exp2(m - m_new) * acc + pv
            m = m_new
        finish(c, acc)

    for c in range(nq):
        prepare(c)
    for c in range(nq):
        query_tile(c)


def _attn(k, qt, vt, *, mode, g0, ngroups, extra=(), lam_init=0.0):
    b, s, _ = k.shape
    nmaps = 4 if mode == "diff" else 2
    mcols = nmaps * TQ
    in_specs = [
        pl.BlockSpec((1, LANES, s), lambda bi, g: (bi, g0 + g, 0)),
        pl.BlockSpec((1, s, LANES), lambda bi, g: (bi, 0, g0 + g)),
        pl.BlockSpec((1, LANES, s), lambda bi, g: (bi, g0 + g, 0)),
    ] + [_resident((1, LANES)) for _ in extra]
    nq = s // TQ
    scratch = [pltpu.VMEM((nq, LANES, mcols), BF16)]
    if mode == "moba":
        scratch += [pltpu.VMEM((BF16_SUBLANES, LANES), F32), pltpu.VMEM((nq, BF16_SUBLANES, mcols), F32)]
    if mode == "dil":
        scratch.append(pltpu.VMEM((N_BIAS, TK, TQ), F32))
    return pl.pallas_call(
        functools.partial(_attn_body, mode=mode, nmaps=nmaps, lam_init=lam_init),
        out_shape=jax.ShapeDtypeStruct((b, s, ngroups * LANES), BF16),
        grid=(b, ngroups),
        in_specs=in_specs,
        out_specs=pl.BlockSpec((1, s, LANES), lambda bi, g: (bi, 0, g)),
        scratch_shapes=scratch,
        compiler_params=_params("arbitrary", "arbitrary"),
        name="attn_" + mode,
    )(qt, k, vt, *extra)


def _memkv_body(m_ref, g_ref, w_ref, o_ref):
    h = _rms(m_ref[0], g_ref[...]).astype(BF16)
    o_ref[0] = jnp.dot(h, w_ref[...], preferred_element_type=F32).astype(BF16)


def _memkv(mem, g, w_kv):
    b, m, d = mem.shape
    n = w_kv.shape[1]
    return pl.pallas_call(
        _memkv_body,
        out_shape=jax.ShapeDtypeStruct((b, m, n), BF16),
        grid=(b,),
        in_specs=[pl.BlockSpec((1, m, d), lambda i: (i, 0, 0)), _resident((1, d)), _resident((d, n))],
        out_specs=pl.BlockSpec((1, m, n), lambda i: (i, 0, 0)),
        compiler_params=_params("parallel"),
        name="mem_kv",
    )(mem, g, w_kv)


def _post_body(x_ref, oa_ref, ob_ref, oc_ref, woa_ref, wob_ref, woc_ref,
               g_ref, wq_ref, kv_ref, wo_ref, o_ref):
    x = x_ref[...]
    x = x + jnp.dot(oa_ref[...], woa_ref[...], preferred_element_type=F32)
    x = x + jnp.dot(ob_ref[...], wob_ref[...], preferred_element_type=F32)
    x = x + jnp.dot(oc_ref[...], woc_ref[...], preferred_element_type=F32)

    h = _rms(x, g_ref[...]).astype(BF16)
    q = jnp.dot(h, wq_ref[...], preferred_element_type=F32) * (HEAD_DIM ** -0.5)
    tm = q.shape[0]
    lane = lax.broadcasted_iota(jnp.int32, (tm, LANES), 1)
    outs = []
    for gi in range(MEM_WIDTH // LANES):
        qg = q[:, gi * LANES:(gi + 1) * LANES]
        qm = jnp.concatenate([jnp.where(lane < HEAD_DIM, qg, 0.0),
                              jnp.where(lane >= HEAD_DIM, qg, 0.0)], axis=0).astype(BF16)
        kg = kv_ref[0, :, gi * LANES:(gi + 1) * LANES]
        vg = kv_ref[0, :, MEM_WIDTH + gi * LANES:MEM_WIDTH + (gi + 1) * LANES]
        s = lax.dot_general(qm, kg, (((1,), (1,)), ((), ())), preferred_element_type=F32)
        p = jnp.exp(s - jnp.max(s, axis=1, keepdims=True))
        l = jnp.sum(p, axis=1, keepdims=True)
        o = jnp.dot(p.astype(BF16), vg, preferred_element_type=F32) / l
        outs.append(jnp.where(lane < HEAD_DIM, o[0:tm], o[tm:2 * tm]).astype(BF16))
    o_mem = jnp.concatenate(outs, axis=1)
    o_ref[...] = x + jnp.dot(o_mem, wo_ref[...], preferred_element_type=F32)


def _post(x, oa, ob, oc, woa, wob, woc, g, wq, kv, wo, seq):
    t, d = x.shape
    tiles_per_seq = seq // TM
    tile = lambda w: pl.BlockSpec((TM, w), lambda i: (i, 0))
    m, n = kv.shape[1], kv.shape[2]
    return pl.pallas_call(
        _post_body,
        out_shape=jax.ShapeDtypeStruct((t, d), F32),
        grid=(t // TM,),
        in_specs=[tile(d), tile(oa.shape[1]), tile(ob.shape[1]), tile(oc.shape[1]),
                  _resident(woa.shape), _resident(wob.shape), _resident(woc.shape),
                  _resident((1, d)), _resident(wq.shape),
                  pl.BlockSpec((1, m, n), lambda i: (i // tiles_per_seq, 0, 0)),
                  _resident(wo.shape)],
        out_specs=tile(d),
        compiler_params=_params("parallel"),
        name="mix_out_mem",
    )(x, oa, ob, oc, woa, wob, woc, g, wq, kv, wo)


def _rotary_tables(positions, head_width):
    rot = head_width // ROPE_FRACTION
    inv_freq = ROPE_THETA ** (-jnp.arange(0, rot, 2, dtype=F32) / rot)
    ang = positions.astype(F32).reshape(-1, 1) * inv_freq
    cos, sin = jnp.cos(ang), jnp.sin(ang)
    t = ang.shape[0]
    rest = head_width - rot
    c_unit = jnp.concatenate([cos, cos, jnp.ones((t, rest), F32)], axis=-1)
    s_unit = jnp.concatenate([-sin, sin, jnp.zeros((t, rest), F32)], axis=-1)
    reps = LANES // head_width
    return jnp.tile(c_unit, (1, reps)), jnp.tile(s_unit, (1, reps))


def kernel(x, mem, positions, ffn1_norm, ffn1_w_gate, ffn1_w_up, ffn1_w_down, mix_norm, w_in, w_out,
           diff_lambda_q1, diff_lambda_k1, diff_lambda_q2, diff_lambda_k2, diff_subln,
           memq_norm, memkv_norm, mem_w_q, mem_w_kv, mem_w_o,
           ffn2_norm, ffn2_w_gate, ffn2_w_up, ffn2_w_down, final_norm):
    b, s, d = x.shape
    depth = w_in.shape[0]
    assert d == D_MODEL and s % TM == 0 and TM % TQ == 0 and TQ == TK == MOBA_BLOCK
    assert s <= DILATED_BRANCHES[-1][0]
    a_w, b_w, c_w = A_HEADS * HEAD_DIM, B_HEADS * HEAD_DIM, C_HEADS * HEAD_DIM

    tabs_n = _rotary_tables(positions, A_QK_DIM) + _rotary_tables(positions, HEAD_DIM)
    tabs_t = tuple(tab.T for tab in tabs_n)
    row = lambda v: v.reshape(1, -1).astype(F32)
    bf = lambda w: w.astype(BF16)

    bounds, c0 = {}, 0
    for name, w in (("qa", a_w), ("ka", a_w), ("va", a_w), ("qb", b_w), ("kb", b_w), ("vb", b_w),
                    ("qc", c_w), ("kc", c_w), ("vc", c_w)):
        bounds[name] = (c0, c0 + w)
        c0 += w
    pick = lambda w, names: jnp.concatenate([w[:, bounds[n][0]:bounds[n][1]] for n in names], axis=1)

    xt = x.reshape(b * s, d)
    for l in range(depth):
        lam_init = 0.8 - 0.6 * math.exp(-0.3 * l)
        lam = (jnp.exp(jnp.sum(diff_lambda_q1[l].astype(F32) * diff_lambda_k1[l].astype(F32)))
               - jnp.exp(jnp.sum(diff_lambda_q2[l].astype(F32) * diff_lambda_k2[l].astype(F32)))
               + lam_init)
        lam_row = jnp.full((1, LANES), lam, F32)
        subln_row = jnp.tile(row(diff_subln[l]), (1, LANES // HEAD_DIM))

        xt = _ffn(xt, row(ffn1_norm[l]), bf(ffn1_w_gate[l]), bf(ffn1_w_up[l]), bf(ffn1_w_down[l]))

        wl = bf(w_in[l])
        k, qt, vt = _proj(xt.reshape(b, s, d), row(mix_norm[l]), pick(wl, ("ka", "kb", "kc")),
                          pick(wl, ("qa", "qb", "qc")).T, pick(wl, ("va", "vb", "vc")).T, tabs_n, tabs_t)
        oa = _attn(k, qt, vt, mode="diff", g0=0, ngroups=A_GROUPS, extra=(lam_row, subln_row), lam_init=lam_init)
        ob = _attn(k, qt, vt, mode="dil", g0=A_GROUPS, ngroups=B_GROUPS)
        oc = _attn(k, qt, vt, mode="moba", g0=A_GROUPS + B_GROUPS, ngroups=C_GROUPS)

        kv = _memkv(mem, row(memkv_norm[l]), bf(mem_w_kv[l]))
        wo = bf(w_out[l])
        xt = _post(xt, oa.reshape(b * s, -1), ob.reshape(b * s, -1), oc.reshape(b * s, -1),
                   wo[:a_w], wo[a_w:a_w + b_w], wo[a_w + b_w:],
                   row(memq_norm[l]), bf(mem_w_q[l]), kv, bf(mem_w_o[l]), s)

        last = l == depth - 1
        xt = _ffn(xt, row(ffn2_norm[l]), bf(ffn2_w_gate[l]), bf(ffn2_w_up[l]), bf(ffn2_w_down[l]),
                  final_g=row(final_norm) if last else None)
    return xt.reshape(b, s, d)
```

```python
import functools
import math

import jax
import jax.numpy as jnp
from jax import lax
from jax.experimental import pallas as pl
from jax.experimental.pallas import tpu as pltpu

F32 = jnp.float32
BF16 = jnp.bfloat16

D_MODEL = 1024
HEAD_DIM = 64
A_HEADS, B_HEADS, C_HEADS = 4, 6, 6
A_QK_DIM = HEAD_DIM // 2
ROPE_THETA = 500000.0
ROPE_FRACTION = 4
DILATED_BRANCHES = ((128, 1), (512, 4), (2048, 16))
MOBA_BLOCK = 256
MOBA_TOP_K = 3
MEM_HEADS = 4
MEM_WIDTH = MEM_HEADS * HEAD_DIM
D_FF = 2816
NORM_EPS = 1e-6
NEG = -1e30
LOG2E = math.log2(math.e)

LANES = 128
BF16_SUBLANES = 16
VMEM_LIMIT = 56 * 1024 * 1024

TM = 512
FF_CHUNK = 512
PROJ_CHUNK = 512
TQ = 256
TK = 256
TILE_INTERLEAVE = 8
N_GROUPS = D_MODEL // LANES
A_GROUPS, B_GROUPS, C_GROUPS = 2, 3, 3
VT_ROWS = LANES + BF16_SUBLANES


def _group_kind(g):
    if g < A_GROUPS:
        return "A", A_QK_DIM ** -0.5 * LOG2E
    return "H", HEAD_DIM ** -0.5 * LOG2E


def _rms(x, g):
    return x * lax.rsqrt(jnp.mean(x * x, axis=-1, keepdims=True) + NORM_EPS) * g


def _resident(shape):
    nd = len(shape)
    return pl.BlockSpec(shape, lambda *_: (0,) * nd)


def _params(*sem):
    return pltpu.CompilerParams(dimension_semantics=sem, vmem_limit_bytes=VMEM_LIMIT)


def _ffn_body(*refs, final):
    if final:
        x_ref, g_ref, wg_ref, wu_ref, wd_ref, fg_ref, o_ref = refs
    else:
        x_ref, g_ref, wg_ref, wu_ref, wd_ref, o_ref = refs
    x = x_ref[...]
    h = _rms(x, g_ref[...]).astype(BF16)
    acc = jnp.zeros(x.shape, F32)
    for c0 in range(0, D_FF, FF_CHUNK):
        c1 = min(c0 + FF_CHUNK, D_FF)
        a = jnp.dot(h, wg_ref[:, c0:c1], preferred_element_type=F32)
        u = jnp.dot(h, wu_ref[:, c0:c1], preferred_element_type=F32)
        t = (a / (1.0 + jnp.exp(-a))) * u
        acc = acc + jnp.dot(t.astype(BF16), wd_ref[c0:c1, :], preferred_element_type=F32)
    y = x + 0.5 * acc
    if final:
        y = _rms(y, fg_ref[...])
    o_ref[...] = y


def _ffn(x, g, wg, wu, wd, final_g=None):
    t, d = x.shape
    final = final_g is not None
    in_specs = [
        pl.BlockSpec((TM, d), lambda i: (i, 0)),
        _resident((1, d)),
        _resident((d, D_FF)),
        _resident((d, D_FF)),
        _resident((D_FF, d)),
    ]
    args = [x, g, wg, wu, wd]
    if final:
        in_specs.append(_resident((1, d)))
        args.append(final_g)
    return pl.pallas_call(
        functools.partial(_ffn_body, final=final),
        out_shape=jax.ShapeDtypeStruct((t, d), F32),
        grid=(t // TM,),
        in_specs=in_specs,
        out_specs=pl.BlockSpec((TM, d), lambda i: (i, 0)),
        compiler_params=_params("parallel"),
        name="ffn_final" if final else "ffn",
    )(*args)


def _rotate_half(blk, first, half, axis):
    n = blk.shape[axis]
    return jnp.where(first, pltpu.roll(blk, n - half, axis), pltpu.roll(blk, half, axis))


def _proj_body(x_ref, g_ref, wk_ref, wqt_ref, wvt_ref, ca_ref, sa_ref, ch_ref, sh_ref,
               cat_ref, sat_ref, cht_ref, sht_ref, k_ref, qt_ref, vt_ref):
    h = _rms(x_ref[0], g_ref[...]).astype(BF16)
    tm = h.shape[0]
    half = {"A": A_QK_DIM // ROPE_FRACTION // 2, "H": HEAD_DIM // ROPE_FRACTION // 2}
    width = {"A": A_QK_DIM, "H": HEAD_DIM}
    nt = (((1,), (1,)), ((), ()))
    per_chunk = PROJ_CHUNK // LANES

    lane = lax.broadcasted_iota(jnp.int32, (tm, LANES), 1)
    first_n = {k: (lane & (width[k] - 1)) < half[k] for k in ("A", "H")}
    tab_n = {"A": (ca_ref, sa_ref), "H": (ch_ref, sh_ref)}
    for c in range(N_GROUPS // per_chunk):
        y = jnp.dot(h, wk_ref[:, c * PROJ_CHUNK:(c + 1) * PROJ_CHUNK], preferred_element_type=F32)
        for j in range(per_chunk):
            g = c * per_chunk + j
            kind, _ = _group_kind(g)
            blk = y[:, j * LANES:(j + 1) * LANES]
            cos_ref, sin_ref = tab_n[kind]
            blk = blk * cos_ref[...] + _rotate_half(blk, first_n[kind], half[kind], 1) * sin_ref[...]
            k_ref[0, :, g * LANES:(g + 1) * LANES] = blk.astype(BF16)

    sub = lax.broadcasted_iota(jnp.int32, (LANES, tm), 0)
    first_t = {k: (sub & (width[k] - 1)) < half[k] for k in ("A", "H")}
    tab_t = {"A": (cat_ref, sat_ref), "H": (cht_ref, sht_ref)}
    for c in range(N_GROUPS // per_chunk):
        y = lax.dot_general(wqt_ref[c * PROJ_CHUNK:(c + 1) * PROJ_CHUNK, :], h, nt,
                            preferred_element_type=F32)
        for j in range(per_chunk):
            g = c * per_chunk + j
            kind, scale = _group_kind(g)
            blk = y[j * LANES:(j + 1) * LANES, :]
            cos_ref, sin_ref = tab_t[kind]
            blk = blk * cos_ref[...] + _rotate_half(blk, first_t[kind], half[kind], 0) * sin_ref[...]
            qt_ref[0, g * LANES:(g + 1) * LANES, :] = (blk * scale).astype(BF16)

    for c in range(N_GROUPS // per_chunk):
        y = lax.dot_general(wvt_ref[c * PROJ_CHUNK:(c + 1) * PROJ_CHUNK, :], h, nt, preferred_element_type=F32)
        vt_ref[0, c * PROJ_CHUNK:(c + 1) * PROJ_CHUNK, :] = y.astype(BF16)


def _proj(x, g, wk, wqt, wvt, tabs_n, tabs_t):
    b, s, d = x.shape
    per_seq = s // TM
    tab_n = pl.BlockSpec((TM, LANES), lambda bi, i: (bi * per_seq + i, 0))
    tab_t = pl.BlockSpec((LANES, TM), lambda bi, i: (0, bi * per_seq + i))
    slab = pl.BlockSpec((1, d, TM), lambda bi, i: (bi, 0, i))
    return pl.pallas_call(
        _proj_body,
        out_shape=(jax.ShapeDtypeStruct((b, s, d), BF16),
                   jax.ShapeDtypeStruct((b, d, s), BF16),
                   jax.ShapeDtypeStruct((b, d, s), BF16)),
        grid=(b, per_seq),
        in_specs=[pl.BlockSpec((1, TM, d), lambda bi, i: (bi, i, 0)), _resident((1, d)),
                  _resident((d, d)), _resident((d, d)), _resident((d, d)),
                  tab_n, tab_n, tab_n, tab_n, tab_t, tab_t, tab_t, tab_t],
        out_specs=(pl.BlockSpec((1, TM, d), lambda bi, i: (bi, i, 0)), slab, slab),
        compiler_params=_params("parallel", "parallel"),
        name="mix_proj",
    )(x, g, wk, wqt, wvt, *tabs_n, *tabs_t)


def _dilated_bias(offset):
    r = lax.broadcasted_iota(jnp.int32, (TK, TQ), 0)
    c = lax.broadcasted_iota(jnp.int32, (TK, TQ), 1)
    delta = c - r + offset * TK
    cnt = jnp.zeros((TK, TQ), jnp.int32)
    for window, dilation in DILATED_BRANCHES:
        hit = (delta <= window) & ((delta & (dilation - 1)) == 0)
        cnt = cnt + jnp.where(hit, 1, 0)
    cnt = jnp.where(delta >= 0, cnt, 0)
    return jnp.where(cnt == 3, math.log2(3.0),
                     jnp.where(cnt == 2, 1.0, jnp.where(cnt == 1, 0.0, NEG))).astype(F32)


N_BIAS = 4


def _attn_body(*refs, mode, nmaps, lam_init):
    if mode == "diff":
        qt_ref, k_ref, vt_ref, lam_ref, sg_ref, o_ref, qm_scr = refs
    elif mode == "moba":
        qt_ref, k_ref, vt_ref, o_ref, qm_scr, km_scr, sel_scr = refs
    else:
        qt_ref, k_ref, vt_ref, o_ref, qm_scr, bias_scr = refs
    nq = k_ref.shape[1] // TQ
    mcols = nmaps * TQ
    width = LANES // nmaps

    if mode == "dil":
        @pl.when((pl.program_id(0) == 0) & (pl.program_id(1) == 0))
        def _():
            for off in range(N_BIAS):
                bias_scr[off] = _dilated_bias(off)

    if mode == "moba":
        km_scr[...] = jnp.zeros(km_scr.shape, F32)
        for j in range(nq):
            kb = k_ref[0, j * MOBA_BLOCK:(j + 1) * MOBA_BLOCK, :].astype(F32)
            km_scr[j:j + 1, :] = jnp.mean(kb, axis=0, keepdims=True)
        km = km_scr[...]
        km_hi = km.astype(BF16)
        km_lo = (km - km_hi.astype(F32)).astype(BF16)

    def prepare(c):
        sub_q = lax.broadcasted_iota(jnp.int32, (LANES, TQ), 0)
        q32 = qt_ref[0, :, c * TQ:(c + 1) * TQ].astype(F32)
        for m in range(nmaps):
            keep = (sub_q >= m * width) & (sub_q < (m + 1) * width)
            qm_scr[c, :, m * TQ:(m + 1) * TQ] = jnp.where(keep, q32, 0.0).astype(BF16)
        if mode == "moba" and c > 0:
            qm = qm_scr[c]
            gate = (jnp.dot(km_hi, qm, preferred_element_type=F32)
                    + jnp.dot(km_lo, qm, preferred_element_type=F32))
            blk_id = lax.broadcasted_iota(jnp.int32, gate.shape, 0)
            rank = jnp.zeros(gate.shape, F32)
            for jp in range(c):
                rowj = gate[jp:jp + 1, :]
                beats = (rowj > gate) | ((rowj == gate) & (blk_id > jp))
                rank = rank + jnp.where(beats, 1.0, 0.0)
            sel_scr[c] = jnp.where((rank < MOBA_TOP_K - 0.5) & (blk_id < c), 1.0, 0.0)

    def finish(c, acc):
        sub_o = lax.broadcasted_iota(jnp.int32, (LANES, TQ), 0)
        outs = [acc[0:LANES, m * TQ:(m + 1) * TQ] / acc[LANES:LANES + 1, m * TQ:(m + 1) * TQ]
                for m in range(nmaps)]
        if mode == "diff":
            lam = lam_ref[0:1, 0:1]
            ot = jnp.where(sub_o < HEAD_DIM, outs[0] - lam * outs[1], outs[2] - lam * outs[3])
        else:
            ot = jnp.where(sub_o < HEAD_DIM, outs[0], outs[1])
        o = ot.T
        if mode == "diff":
            lane_o = lax.broadcasted_iota(jnp.int32, (TQ, LANES), 1)
            sq = o * o
            ss0 = jnp.sum(jnp.where(lane_o < HEAD_DIM, sq, 0.0), axis=1, keepdims=True)
            ss1 = jnp.sum(jnp.where(lane_o >= HEAD_DIM, sq, 0.0), axis=1, keepdims=True)
            var = jnp.where(lane_o < HEAD_DIM, ss0, ss1) * (1.0 / HEAD_DIM)
            o = o * lax.rsqrt(var + NORM_EPS) * sg_ref[...] * (1.0 - lam_init)
        o_ref[0, c * TQ:(c + 1) * TQ, :] = o.astype(BF16)

    ones_rows = jnp.ones((BF16_SUBLANES, TK), BF16)

    def scores(c, b):
        s = jnp.dot(k_ref[0, b * TK:(b + 1) * TK, :], qm_scr[c], preferred_element_type=F32)
        if mode == "dil":
            return s + jnp.concatenate([bias_scr[min(c - b, N_BIAS - 1)]] * nmaps, axis=1)
        if b == c:
            kpos = lax.broadcasted_iota(jnp.int32, (TK, mcols), 0)
            qpos = lax.broadcasted_iota(jnp.int32, (TK, mcols), 1) & (TQ - 1)
            return jnp.where(kpos <= qpos, s, NEG)
        return s

    def query_tile(c):
        order = [c] + list(range(c))
        s_next = scores(c, order[0])
        m = acc = None
        for idx, b in enumerate(order):
            yield
            s = s_next
            if idx + 1 < len(order):
                s_next = scores(c, order[idx + 1])
            top = jnp.max(s, axis=0, keepdims=True)
            gated = mode == "moba" and b < c
            if gated:
                chosen = sel_scr[c, b:b + 1, :] > 0.5
                top = jnp.where(chosen, top, NEG)
            m_new = top if m is None else jnp.maximum(m, top)
            sub = jnp.where(chosen, m_new, -NEG) if gated else m_new
            p = jnp.exp2(s - sub).astype(BF16)
            vt = jnp.concatenate([vt_ref[0, :, b * TK:(b + 1) * TK], ones_rows], axis=0)
            pv = jnp.dot(vt, p, preferred_element_type=F32)
            acc = pv if acc is None else jnp.exp2(m - m_new) * acc + pv
            m = m_new
        finish(c, acc)

    for c in range(nq):
        prepare(c)
    tiles = list(range(nq - 1, -1, -1))
    for g0 in range(0, nq, TILE_INTERLEAVE):
        live = [query_tile(c) for c in tiles[g0:g0 + TILE_INTERLEAVE]]
        while live:
            live = [gen for gen in live if next(gen, "done") != "done"]


def _attn(k, qt, vt, *, mode, g0, ngroups, extra=(), lam_init=0.0):
    b, s, _ = k.shape
    nmaps = 4 if mode == "diff" else 2
    mcols = nmaps * TQ
    in_specs = [
        pl.BlockSpec((1, LANES, s), lambda bi, g: (bi, g0 + g, 0)),
        pl.BlockSpec((1, s, LANES), lambda bi, g: (bi, 0, g0 + g)),
        pl.BlockSpec((1, LANES, s), lambda bi, g: (bi, g0 + g, 0)),
    ] + [_resident((1, LANES)) for _ in extra]
    nq = s // TQ
    scratch = [pltpu.VMEM((nq, LANES, mcols), BF16)]
    if mode == "moba":
        scratch += [pltpu.VMEM((BF16_SUBLANES, LANES), F32), pltpu.VMEM((nq, BF16_SUBLANES, mcols), F32)]
    if mode == "dil":
        scratch.append(pltpu.VMEM((N_BIAS, TK, TQ), F32))
    return pl.pallas_call(
        functools.partial(_attn_body, mode=mode, nmaps=nmaps, lam_init=lam_init),
        out_shape=jax.ShapeDtypeStruct((b, s, ngroups * LANES), BF16),
        grid=(b, ngroups),
        in_specs=in_specs,
        out_specs=pl.BlockSpec((1, s, LANES), lambda bi, g: (bi, 0, g)),
        scratch_shapes=scratch,
        compiler_params=_params("arbitrary", "arbitrary"),
        name="attn_" + mode,
    )(qt, k, vt, *extra)


def _memkv_body(m_ref, g_ref, w_ref, o_ref):
    h = _rms(m_ref[0], g_ref[...]).astype(BF16)
    o_ref[0] = jnp.dot(h, w_ref[...], preferred_element_type=F32).astype(BF16)


def _memkv(mem, g, w_kv):
    b, m, d = mem.shape
    n = w_kv.shape[1]
    return pl.pallas_call(
        _memkv_body,
        out_shape=jax.ShapeDtypeStruct((b, m, n), BF16),
        grid=(b,),
        in_specs=[pl.BlockSpec((1, m, d), lambda i: (i, 0, 0)), _resident((1, d)), _resident((d, n))],
        out_specs=pl.BlockSpec((1, m, n), lambda i: (i, 0, 0)),
        compiler_params=_params("parallel"),
        name="mem_kv",
    )(mem, g, w_kv)


def _post_body(x_ref, oa_ref, ob_ref, oc_ref, woa_ref, wob_ref, woc_ref,
               g_ref, wq_ref, kv_ref, wo_ref, o_ref):
    x = x_ref[...]
    x = x + jnp.dot(oa_ref[...], woa_ref[...], preferred_element_type=F32)
    x = x + jnp.dot(ob_ref[...], wob_ref[...], preferred_element_type=F32)
    x = x + jnp.dot(oc_ref[...], woc_ref[...], preferred_element_type=F32)

    h = _rms(x, g_ref[...]).astype(BF16)
    q = jnp.dot(h, wq_ref[...], preferred_element_type=F32) * (HEAD_DIM ** -0.5)
    tm = q.shape[0]
    lane = lax.broadcasted_iota(jnp.int32, (tm, LANES), 1)
    outs = []
    for gi in range(MEM_WIDTH // LANES):
        qg = q[:, gi * LANES:(gi + 1) * LANES]
        qm = jnp.concatenate([jnp.where(lane < HEAD_DIM, qg, 0.0),
                              jnp.where(lane >= HEAD_DIM, qg, 0.0)], axis=0).astype(BF16)
        kg = kv_ref[0, :, gi * LANES:(gi + 1) * LANES]
        vg = kv_ref[0, :, MEM_WIDTH + gi * LANES:MEM_WIDTH + (gi + 1) * LANES]
        s = lax.dot_general(qm, kg, (((1,), (1,)), ((), ())), preferred_element_type=F32)
        p = jnp.exp(s - jnp.max(s, axis=1, keepdims=True))
        l = jnp.sum(p, axis=1, keepdims=True)
        o = jnp.dot(p.astype(BF16), vg, preferred_element_type=F32) / l
        outs.append(jnp.where(lane < HEAD_DIM, o[0:tm], o[tm:2 * tm]).astype(BF16))
    o_mem = jnp.concatenate(outs, axis=1)
    o_ref[...] = x + jnp.dot(o_mem, wo_ref[...], preferred_element_type=F32)


def _post(x, oa, ob, oc, woa, wob, woc, g, wq, kv, wo, seq):
    t, d = x.shape
    tiles_per_seq = seq // TM
    tile = lambda w: pl.BlockSpec((TM, w), lambda i: (i, 0))
    m, n = kv.shape[1], kv.shape[2]
    return pl.pallas_call(
        _post_body,
        out_shape=jax.ShapeDtypeStruct((t, d), F32),
        grid=(t // TM,),
        in_specs=[tile(d), tile(oa.shape[1]), tile(ob.shape[1]), tile(oc.shape[1]),
                  _resident(woa.shape), _resident(wob.shape), _resident(woc.shape),
                  _resident((1, d)), _resident(wq.shape),
                  pl.BlockSpec((1, m, n), lambda i: (i // tiles_per_seq, 0, 0)),
                  _resident(wo.shape)],
        out_specs=tile(d),
        compiler_params=_params("parallel"),
        name="mix_out_mem",
    )(x, oa, ob, oc, woa, wob, woc, g, wq, kv, wo)


def _rotary_tables(positions, head_width, transposed):
    rot = head_width // ROPE_FRACTION
    inv_freq = ROPE_THETA ** (-jnp.arange(0, rot, 2, dtype=F32) / rot)
    pos = positions.astype(F32).reshape(-1)
    t = pos.shape[0]
    rest = head_width - rot
    reps = LANES // head_width
    if transposed:
        ang = inv_freq[:, None] * pos[None, :]
        cos, sin = jnp.cos(ang), jnp.sin(ang)
        c_unit = jnp.concatenate([cos, cos, jnp.ones((rest, t), F32)], axis=0)
        s_unit = jnp.concatenate([-sin, sin, jnp.zeros((rest, t), F32)], axis=0)
        return jnp.tile(c_unit, (reps, 1)), jnp.tile(s_unit, (reps, 1))
    ang = pos[:, None] * inv_freq
    cos, sin = jnp.cos(ang), jnp.sin(ang)
    c_unit = jnp.concatenate([cos, cos, jnp.ones((t, rest), F32)], axis=-1)
    s_unit = jnp.concatenate([-sin, sin, jnp.zeros((t, rest), F32)], axis=-1)
    return jnp.tile(c_unit, (1, reps)), jnp.tile(s_unit, (1, reps))


def kernel(x, mem, positions, ffn1_norm, ffn1_w_gate, ffn1_w_up, ffn1_w_down, mix_norm, w_in, w_out,
           diff_lambda_q1, diff_lambda_k1, diff_lambda_q2, diff_lambda_k2, diff_subln,
           memq_norm, memkv_norm, mem_w_q, mem_w_kv, mem_w_o,
           ffn2_norm, ffn2_w_gate, ffn2_w_up, ffn2_w_down, final_norm):
    b, s, d = x.shape
    depth = w_in.shape[0]
    assert d == D_MODEL and s % TM == 0 and TM % TQ == 0 and TQ == TK == MOBA_BLOCK
    assert s <= DILATED_BRANCHES[-1][0]
    a_w, b_w, c_w = A_HEADS * HEAD_DIM, B_HEADS * HEAD_DIM, C_HEADS * HEAD_DIM

    tabs_n = _rotary_tables(positions, A_QK_DIM, False) + _rotary_tables(positions, HEAD_DIM, False)
    tabs_t = _rotary_tables(positions, A_QK_DIM, True) + _rotary_tables(positions, HEAD_DIM, True)
    row = lambda v: v.reshape(1, -1).astype(F32)
    bf = lambda w: w.astype(BF16)

    bounds, c0 = {}, 0
    for name, w in (("qa", a_w), ("ka", a_w), ("va", a_w), ("qb", b_w), ("kb", b_w), ("vb", b_w),
                    ("qc", c_w), ("kc", c_w), ("vc", c_w)):
        bounds[name] = (c0, c0 + w)
        c0 += w
    pick = lambda w, names: jnp.concatenate([w[:, bounds[n][0]:bounds[n][1]] for n in names], axis=1)

    xt = x.reshape(b * s, d)
    for l in range(depth):
        lam_init = 0.8 - 0.6 * math.exp(-0.3 * l)
        lam = (jnp.exp(jnp.sum(diff_lambda_q1[l].astype(F32) * diff_lambda_k1[l].astype(F32)))
               - jnp.exp(jnp.sum(diff_lambda_q2[l].astype(F32) * diff_lambda_k2[l].astype(F32)))
               + lam_init)
        lam_row = jnp.full((1, LANES), lam, F32)
        subln_row = jnp.tile(row(diff_subln[l]), (1, LANES // HEAD_DIM))

        xt = _ffn(xt, row(ffn1_norm[l]), bf(ffn1_w_gate[l]), bf(ffn1_w_up[l]), bf(ffn1_w_down[l]))

        wl = bf(w_in[l])
        k, qt, vt = _proj(xt.reshape(b, s, d), row(mix_norm[l]), pick(wl, ("ka", "kb", "kc")),
                          pick(wl, ("qa", "qb", "qc")).T, pick(wl, ("va", "vb", "vc")).T, tabs_n, tabs_t)
        oa = _attn(k, qt, vt, mode="diff", g0=0, ngroups=A_GROUPS, extra=(lam_row, subln_row), lam_init=lam_init)
        ob = _attn(k, qt, vt, mode="dil", g0=A_GROUPS, ngroups=B_GROUPS)
        oc = _attn(k, qt, vt, mode="moba", g0=A_GROUPS + B_GROUPS, ngroups=C_GROUPS)

        kv = _memkv(mem, row(memkv_norm[l]), bf(mem_w_kv[l]))
        wo = bf(w_out[l])
        xt = _post(xt, oa.reshape(b * s, -1), ob.reshape(b * s, -1), oc.reshape(b * s, -1),
                   wo[:a_w], wo[a_w:a_w + b_w], wo[a_w + b_w:],
                   row(memq_norm[l]), bf(mem_w_q[l]), kv, bf(mem_w_o[l]), s)

        last = l == depth - 1
        xt = _ffn(xt, row(ffn2_norm[l]), bf(ffn2_w_gate[l]), bf(ffn2_w_up[l]), bf(ffn2_w_down[l]),
                  final_g=row(final_norm) if last else None)
    return xt.reshape(b, s, d)
```

```python
import functools
import math

import jax
import jax.numpy as jnp
from jax import lax
from jax.experimental import pallas as pl
from jax.experimental.pallas import tpu as pltpu

F32 = jnp.float32
BF16 = jnp.bfloat16

D_MODEL = 1024
HEAD_DIM = 64
A_HEADS, B_HEADS, C_HEADS = 4, 6, 6
A_QK_DIM = HEAD_DIM // 2
ROPE_THETA = 500000.0
ROPE_FRACTION = 4
DILATED_BRANCHES = ((128, 1), (512, 4), (2048, 16))
MOBA_BLOCK = 256
MOBA_TOP_K = 3
MEM_HEADS = 4
MEM_WIDTH = MEM_HEADS * HEAD_DIM
D_FF = 2816
NORM_EPS = 1e-6
NEG = -1e30
LOG2E = math.log2(math.e)

LANES = 128
BF16_SUBLANES = 16
VMEM_LIMIT = 56 * 1024 * 1024

TM = 512
FF_CHUNK = 512
TQ = 256
TK = 256
TILE_INTERLEAVE = 8
N_GROUPS = D_MODEL // LANES
A_GROUPS, B_GROUPS, C_GROUPS = 2, 3, 3
VT_ROWS = LANES + BF16_SUBLANES


def _group_kind(g):
    if g < A_GROUPS:
        return "A", A_QK_DIM ** -0.5 * LOG2E
    return "H", HEAD_DIM ** -0.5 * LOG2E


def _rms(x, g):
    return x * lax.rsqrt(jnp.mean(x * x, axis=-1, keepdims=True) + NORM_EPS) * g


def _resident(shape):
    nd = len(shape)
    return pl.BlockSpec(shape, lambda *_: (0,) * nd)


def _params(*sem):
    return pltpu.CompilerParams(dimension_semantics=sem, vmem_limit_bytes=VMEM_LIMIT)


def _ffn_body(*refs, final):
    if final:
        x_ref, g_ref, wg_ref, wu_ref, wd_ref, fg_ref, o_ref = refs
    else:
        x_ref, g_ref, wg_ref, wu_ref, wd_ref, o_ref = refs
    x = x_ref[...]
    h = _rms(x, g_ref[...]).astype(BF16)
    acc = jnp.zeros(x.shape, F32)
    for c0 in range(0, D_FF, FF_CHUNK):
        c1 = min(c0 + FF_CHUNK, D_FF)
        a = jnp.dot(h, wg_ref[:, c0:c1], preferred_element_type=F32)
        u = jnp.dot(h, wu_ref[:, c0:c1], preferred_element_type=F32)
        t = (a / (1.0 + jnp.exp(-a))) * u
        acc = acc + jnp.dot(t.astype(BF16), wd_ref[c0:c1, :], preferred_element_type=F32)
    y = x + 0.5 * acc
    if final:
        y = _rms(y, fg_ref[...])
    o_ref[...] = y


def _ffn(x, g, wg, wu, wd, final_g=None):
    t, d = x.shape
    final = final_g is not None
    in_specs = [
        pl.BlockSpec((TM, d), lambda i: (i, 0)),
        _resident((1, d)),
        _resident((d, D_FF)),
        _resident((d, D_FF)),
        _resident((D_FF, d)),
    ]
    args = [x, g, wg, wu, wd]
    if final:
        in_specs.append(_resident((1, d)))
        args.append(final_g)
    return pl.pallas_call(
        functools.partial(_ffn_body, final=final),
        out_shape=jax.ShapeDtypeStruct((t, d), F32),
        grid=(t // TM,),
        in_specs=in_specs,
        out_specs=pl.BlockSpec((TM, d), lambda i: (i, 0)),
        compiler_params=_params("parallel"),
        name="ffn_final" if final else "ffn",
    )(*args)


def _rotate_half(blk, first, half, axis):
    n = blk.shape[axis]
    return jnp.where(first, pltpu.roll(blk, n - half, axis), pltpu.roll(blk, half, axis))


def _wt_row(part, g):
    idx = "qkv".index(part)
    a_w, b_w, c_w = A_HEADS * HEAD_DIM, B_HEADS * HEAD_DIM, C_HEADS * HEAD_DIM
    if g < A_GROUPS:
        return idx * a_w + g * LANES
    if g < A_GROUPS + B_GROUPS:
        return 3 * a_w + idx * b_w + (g - A_GROUPS) * LANES
    return 3 * (a_w + b_w) + idx * c_w + (g - A_GROUPS - B_GROUPS) * LANES


def _proj_body(x_ref, g_ref, wt_ref, ca_ref, sa_ref, ch_ref, sh_ref, k_ref, qt_ref, vt_ref):
    h = _rms(x_ref[0], g_ref[...]).astype(BF16)
    tm = h.shape[0]
    nt = (((1,), (1,)), ((), ()))
    sub = lax.broadcasted_iota(jnp.int32, (LANES, tm), 0)
    tables = {}
    for kind, width, cos_ref, sin_ref in (("A", A_QK_DIM, ca_ref, sa_ref), ("H", HEAD_DIM, ch_ref, sh_ref)):
        rot = width // ROPE_FRACTION
        plain = width - rot
        reps = LANES // width
        cos_t = jnp.concatenate([cos_ref[...], jnp.ones((plain, tm), F32)] * reps, axis=0)
        sin_t = jnp.concatenate([sin_ref[...], jnp.zeros((plain, tm), F32)] * reps, axis=0)
        tables[kind] = (cos_t, sin_t, (sub & (width - 1)) < rot // 2, rot // 2)

    families = ((0, A_GROUPS), (A_GROUPS, A_GROUPS + B_GROUPS), (A_GROUPS + B_GROUPS, N_GROUPS))
    for part in "qkv":
        for g_lo, g_hi in families:
            r0 = _wt_row(part, g_lo)
            y = lax.dot_general(wt_ref[r0:r0 + (g_hi - g_lo) * LANES, :], h, nt,
                                preferred_element_type=F32)
            for g in range(g_lo, g_hi):
                blk = y[(g - g_lo) * LANES:(g - g_lo + 1) * LANES, :]
                cols = slice(g * LANES, (g + 1) * LANES)
                if part == "v":
                    vt_ref[0, cols, :] = blk.astype(BF16)
                    continue
                kind, scale = _group_kind(g)
                cos_t, sin_t, first, half = tables[kind]
                blk = blk * cos_t + _rotate_half(blk, first, half, 0) * sin_t
                if part == "q":
                    qt_ref[0, cols, :] = (blk * scale).astype(BF16)
                else:
                    k_ref[0, :, cols] = blk.T.astype(BF16)


def _proj(x, g, wt, tabs):
    b, s, d = x.shape
    per_seq = s // TM
    tab = lambda rows: pl.BlockSpec((rows, TM), lambda bi, i: (0, bi * per_seq + i))
    slab = pl.BlockSpec((1, d, TM), lambda bi, i: (bi, 0, i))
    return pl.pallas_call(
        _proj_body,
        out_shape=(jax.ShapeDtypeStruct((b, s, d), BF16),
                   jax.ShapeDtypeStruct((b, d, s), BF16),
                   jax.ShapeDtypeStruct((b, d, s), BF16)),
        grid=(b, per_seq),
        in_specs=[pl.BlockSpec((1, TM, d), lambda bi, i: (bi, i, 0)), _resident((1, d)), _resident(wt.shape)]
                 + [tab(t.shape[0]) for t in tabs],
        out_specs=(pl.BlockSpec((1, TM, d), lambda bi, i: (bi, i, 0)), slab, slab),
        compiler_params=_params("parallel", "parallel"),
        name="mix_proj",
    )(x, g, wt, *tabs)


def _dilated_bias(offset):
    r = lax.broadcasted_iota(jnp.int32, (TK, TQ), 0)
    c = lax.broadcasted_iota(jnp.int32, (TK, TQ), 1)
    delta = c - r + offset * TK
    cnt = jnp.zeros((TK, TQ), jnp.int32)
    for window, dilation in DILATED_BRANCHES:
        hit = (delta <= window) & ((delta & (dilation - 1)) == 0)
        cnt = cnt + jnp.where(hit, 1, 0)
    cnt = jnp.where(delta >= 0, cnt, 0)
    return jnp.where(cnt == 3, math.log2(3.0),
                     jnp.where(cnt == 2, 1.0, jnp.where(cnt == 1, 0.0, NEG))).astype(F32)


N_BIAS = 4


def _attn_body(*refs, mode, nmaps, lam_init):
    if mode == "diff":
        qt_ref, k_ref, vt_ref, lam_ref, sg_ref, o_ref, qm_scr = refs
    elif mode == "moba":
        qt_ref, k_ref, vt_ref, o_ref, qm_scr, km_scr, sel_scr = refs
    else:
        qt_ref, k_ref, vt_ref, o_ref, qm_scr, bias_scr = refs
    nq = k_ref.shape[1] // TQ
    mcols = nmaps * TQ
    width = LANES // nmaps

    if mode == "dil":
        @pl.when((pl.program_id(0) == 0) & (pl.program_id(1) == 0))
        def _():
            for off in range(N_BIAS):
                bias_scr[off] = _dilated_bias(off)

    if mode == "moba":
        km_scr[...] = jnp.zeros(km_scr.shape, F32)
        for j in range(nq):
            kb = k_ref[0, j * MOBA_BLOCK:(j + 1) * MOBA_BLOCK, :].astype(F32)
            km_scr[j:j + 1, :] = jnp.mean(kb, axis=0, keepdims=True)
        km = km_scr[...]
        km_hi = km.astype(BF16)
        km_lo = (km - km_hi.astype(F32)).astype(BF16)

    def prepare(c):
        sub_q = lax.broadcasted_iota(jnp.int32, (LANES, TQ), 0)
        q32 = qt_ref[0, :, c * TQ:(c + 1) * TQ].astype(F32)
        for m in range(nmaps):
            keep = (sub_q >= m * width) & (sub_q < (m + 1) * width)
            qm_scr[c, :, m * TQ:(m + 1) * TQ] = jnp.where(keep, q32, 0.0).astype(BF16)
        if mode == "moba" and c > 0:
            qm = qm_scr[c]
            gate = (jnp.dot(km_hi, qm, preferred_element_type=F32)
                    + jnp.dot(km_lo, qm, preferred_element_type=F32))
            blk_id = lax.broadcasted_iota(jnp.int32, gate.shape, 0)
            rank = jnp.zeros(gate.shape, F32)
            for jp in range(c):
                rowj = gate[jp:jp + 1, :]
                beats = (rowj > gate) | ((rowj == gate) & (blk_id > jp))
                rank = rank + jnp.where(beats, 1.0, 0.0)
            sel_scr[c] = jnp.where((rank < MOBA_TOP_K - 0.5) & (blk_id < c), 1.0, 0.0)

    def finish(c, acc):
        sub_o = lax.broadcasted_iota(jnp.int32, (LANES, TQ), 0)
        outs = [acc[0:LANES, m * TQ:(m + 1) * TQ] / acc[LANES:LANES + 1, m * TQ:(m + 1) * TQ]
                for m in range(nmaps)]
        if mode == "diff":
            lam = lam_ref[0:1, 0:1]
            ot = jnp.where(sub_o < HEAD_DIM, outs[0] - lam * outs[1], outs[2] - lam * outs[3])
        else:
            ot = jnp.where(sub_o < HEAD_DIM, outs[0], outs[1])
        o = ot.T
        if mode == "diff":
            lane_o = lax.broadcasted_iota(jnp.int32, (TQ, LANES), 1)
            sq = o * o
            ss0 = jnp.sum(jnp.where(lane_o < HEAD_DIM, sq, 0.0), axis=1, keepdims=True)
            ss1 = jnp.sum(jnp.where(lane_o >= HEAD_DIM, sq, 0.0), axis=1, keepdims=True)
            var = jnp.where(lane_o < HEAD_DIM, ss0, ss1) * (1.0 / HEAD_DIM)
            o = o * lax.rsqrt(var + NORM_EPS) * sg_ref[...] * (1.0 - lam_init)
        o_ref[0, c * TQ:(c + 1) * TQ, :] = o.astype(BF16)

    ones_rows = jnp.ones((BF16_SUBLANES, TK), BF16)

    def scores(c, b):
        s = jnp.dot(k_ref[0, b * TK:(b + 1) * TK, :], qm_scr[c], preferred_element_type=F32)
        if mode == "dil":
            return s + jnp.concatenate([bias_scr[min(c - b, N_BIAS - 1)]] * nmaps, axis=1)
        if b == c:
            kpos = lax.broadcasted_iota(jnp.int32, (TK, mcols), 0)
            qpos = lax.broadcasted_iota(jnp.int32, (TK, mcols), 1) & (TQ - 1)
            return jnp.where(kpos <= qpos, s, NEG)
        return s

    def query_tile(c):
        order = [c] + list(range(c))
        s_next = scores(c, order[0])
        m = acc = None
        for idx, b in enumerate(order):
            yield
            s = s_next
            if idx + 1 < len(order):
                s_next = scores(c, order[idx + 1])
            top = jnp.max(s, axis=0, keepdims=True)
            gated = mode == "moba" and b < c
            if gated:
                chosen = sel_scr[c, b:b + 1, :] > 0.5
                top = jnp.where(chosen, top, NEG)
            m_new = top if m is None else jnp.maximum(m, top)
            sub = jnp.where(chosen, m_new, -NEG) if gated else m_new
            p = jnp.exp2(s - sub).astype(BF16)
            vt = jnp.concatenate([vt_ref[0, :, b * TK:(b + 1) * TK], ones_rows], axis=0)
            pv = jnp.dot(vt, p, preferred_element_type=F32)
            acc = pv if acc is None else jnp.exp2(m - m_new) * acc + pv
            m = m_new
        finish(c, acc)

    for c in range(nq):
        prepare(c)
    tiles = list(range(nq - 1, -1, -1))
    for g0 in range(0, nq, TILE_INTERLEAVE):
        live = [query_tile(c) for c in tiles[g0:g0 + TILE_INTERLEAVE]]
        while live:
            live = [gen for gen in live if next(gen, "done") != "done"]


def _attn(k, qt, vt, *, mode, g0, ngroups, extra=(), lam_init=0.0):
    b, s, _ = k.shape
    nmaps = 4 if mode == "diff" else 2
    mcols = nmaps * TQ
    in_specs = [
        pl.BlockSpec((1, LANES, s), lambda bi, g: (bi, g0 + g, 0)),
        pl.BlockSpec((1, s, LANES), lambda bi, g: (bi, 0, g0 + g)),
        pl.BlockSpec((1, LANES, s), lambda bi, g: (bi, g0 + g, 0)),
    ] + [_resident((1, LANES)) for _ in extra]
    nq = s // TQ
    scratch = [pltpu.VMEM((nq, LANES, mcols), BF16)]
    if mode == "moba":
        scratch += [pltpu.VMEM((BF16_SUBLANES, LANES), F32), pltpu.VMEM((nq, BF16_SUBLANES, mcols), F32)]
    if mode == "dil":
        scratch.append(pltpu.VMEM((N_BIAS, TK, TQ), F32))
    return pl.pallas_call(
        functools.partial(_attn_body, mode=mode, nmaps=nmaps, lam_init=lam_init),
        out_shape=jax.ShapeDtypeStruct((b, s, ngroups * LANES), BF16),
        grid=(b, ngroups),
        in_specs=in_specs,
        out_specs=pl.BlockSpec((1, s, LANES), lambda bi, g: (bi, 0, g)),
        scratch_shapes=scratch,
        compiler_params=_params("arbitrary", "arbitrary"),
        name="attn_" + mode,
    )(qt, k, vt, *extra)


def _memkv_body(m_ref, g_ref, w_ref, o_ref):
    h = _rms(m_ref[0], g_ref[...]).astype(BF16)
    o_ref[0] = jnp.dot(h, w_ref[...], preferred_element_type=F32).astype(BF16)


def _memkv(mem, g, w_kv):
    b, m, d = mem.shape
    n = w_kv.shape[1]
    return pl.pallas_call(
        _memkv_body,
        out_shape=jax.ShapeDtypeStruct((b, m, n), BF16),
        grid=(b,),
        in_specs=[pl.BlockSpec((1, m, d), lambda i: (i, 0, 0)), _resident((1, d)), _resident((d, n))],
        out_specs=pl.BlockSpec((1, m, n), lambda i: (i, 0, 0)),
        compiler_params=_params("parallel"),
        name="mem_kv",
    )(mem, g, w_kv)


def _post_body(x_ref, oa_ref, ob_ref, oc_ref, wout_ref, g_ref, wq_ref, kv_ref, wo_ref, o_ref):
    x = x_ref[...]
    r0 = 0
    for part_ref in (oa_ref, ob_ref, oc_ref):
        r1 = r0 + part_ref.shape[1]
        x = x + jnp.dot(part_ref[...], wout_ref[r0:r1, :], preferred_element_type=F32)
        r0 = r1

    h = _rms(x, g_ref[...]).astype(BF16)
    q = jnp.dot(h, wq_ref[...], preferred_element_type=F32) * (HEAD_DIM ** -0.5)
    tm = q.shape[0]
    lane = lax.broadcasted_iota(jnp.int32, (tm, LANES), 1)
    outs = []
    for gi in range(MEM_WIDTH // LANES):
        qg = q[:, gi * LANES:(gi + 1) * LANES]
        qm = jnp.concatenate([jnp.where(lane < HEAD_DIM, qg, 0.0),
                              jnp.where(lane >= HEAD_DIM, qg, 0.0)], axis=0).astype(BF16)
        kg = kv_ref[0, :, gi * LANES:(gi + 1) * LANES]
        vg = kv_ref[0, :, MEM_WIDTH + gi * LANES:MEM_WIDTH + (gi + 1) * LANES]
        s = lax.dot_general(qm, kg, (((1,), (1,)), ((), ())), preferred_element_type=F32)
        p = jnp.exp(s - jnp.max(s, axis=1, keepdims=True))
        l = jnp.sum(p, axis=1, keepdims=True)
        o = jnp.dot(p.astype(BF16), vg, preferred_element_type=F32) / l
        outs.append(jnp.where(lane < HEAD_DIM, o[0:tm], o[tm:2 * tm]).astype(BF16))
    o_mem = jnp.concatenate(outs, axis=1)
    o_ref[...] = x + jnp.dot(o_mem, wo_ref[...], preferred_element_type=F32)


def _post(x, oa, ob, oc, wout, g, wq, kv, wo, seq):
    t, d = x.shape
    tiles_per_seq = seq // TM
    tile = lambda w: pl.BlockSpec((TM, w), lambda i: (i, 0))
    m, n = kv.shape[1], kv.shape[2]
    return pl.pallas_call(
        _post_body,
        out_shape=jax.ShapeDtypeStruct((t, d), F32),
        grid=(t // TM,),
        in_specs=[tile(d), tile(oa.shape[1]), tile(ob.shape[1]), tile(oc.shape[1]),
                  _resident(wout.shape),
                  _resident((1, d)), _resident(wq.shape),
                  pl.BlockSpec((1, m, n), lambda i: (i // tiles_per_seq, 0, 0)),
                  _resident(wo.shape)],
        out_specs=tile(d),
        compiler_params=_params("parallel"),
        name="mix_out_mem",
    )(x, oa, ob, oc, wout, g, wq, kv, wo)


def _rotary_tables(positions, head_width):
    rot = head_width // ROPE_FRACTION
    inv_freq = ROPE_THETA ** (-jnp.arange(0, rot, 2, dtype=F32) / rot)
    ang = inv_freq[:, None] * positions.astype(F32).reshape(1, -1)
    cos, sin = jnp.cos(ang), jnp.sin(ang)
    return jnp.concatenate([cos, cos], axis=0), jnp.concatenate([-sin, sin], axis=0)


def kernel(x, mem, positions, ffn1_norm, ffn1_w_gate, ffn1_w_up, ffn1_w_down, mix_norm, w_in, w_out,
           diff_lambda_q1, diff_lambda_k1, diff_lambda_q2, diff_lambda_k2, diff_subln,
           memq_norm, memkv_norm, mem_w_q, mem_w_kv, mem_w_o,
           ffn2_norm, ffn2_w_gate, ffn2_w_up, ffn2_w_down, final_norm):
    b, s, d = x.shape
    depth = w_in.shape[0]
    assert d == D_MODEL and s % TM == 0 and TM % TQ == 0 and TQ == TK == MOBA_BLOCK
    assert s <= DILATED_BRANCHES[-1][0]

    tabs = _rotary_tables(positions, A_QK_DIM) + _rotary_tables(positions, HEAD_DIM)
    row = lambda v: v.reshape(1, -1).astype(F32)
    bf = lambda w: w.astype(BF16)

    xt = x.reshape(b * s, d)
    for l in range(depth):
        lam_init = 0.8 - 0.6 * math.exp(-0.3 * l)
        lam = (jnp.exp(jnp.sum(diff_lambda_q1[l].astype(F32) * diff_lambda_k1[l].astype(F32)))
               - jnp.exp(jnp.sum(diff_lambda_q2[l].astype(F32) * diff_lambda_k2[l].astype(F32)))
               + lam_init)
        lam_row = jnp.full((1, LANES), lam, F32)
        subln_row = jnp.tile(row(diff_subln[l]), (1, LANES // HEAD_DIM))

        xt = _ffn(xt, row(ffn1_norm[l]), bf(ffn1_w_gate[l]), bf(ffn1_w_up[l]), bf(ffn1_w_down[l]))

        k, qt, vt = _proj(xt.reshape(b, s, d), row(mix_norm[l]), bf(w_in[l].T), tabs)
        oa = _attn(k, qt, vt, mode="diff", g0=0, ngroups=A_GROUPS, extra=(lam_row, subln_row), lam_init=lam_init)
        ob = _attn(k, qt, vt, mode="dil", g0=A_GROUPS, ngroups=B_GROUPS)
        oc = _attn(k, qt, vt, mode="moba", g0=A_GROUPS + B_GROUPS, ngroups=C_GROUPS)

        kv = _memkv(mem, row(memkv_norm[l]), bf(mem_w_kv[l]))
        xt = _post(xt, oa.reshape(b * s, -1), ob.reshape(b * s, -1), oc.reshape(b * s, -1), bf(w_out[l]),
                   row(memq_norm[l]), bf(mem_w_q[l]), kv, bf(mem_w_o[l]), s)

        last = l == depth - 1
        xt = _ffn(xt, row(ffn2_norm[l]), bf(ffn2_w_gate[l]), bf(ffn2_w_up[l]), bf(ffn2_w_down[l]),
                  final_g=row(final_norm) if last else None)
    return xt.reshape(b, s, d)
```

```python
import functools
import math

import jax
import jax.numpy as jnp
from jax import lax
from jax.experimental import pallas as pl
from jax.experimental.pallas import tpu as pltpu

F32 = jnp.float32
BF16 = jnp.bfloat16

D_MODEL = 1024
HEAD_DIM = 64
A_HEADS, B_HEADS, C_HEADS = 4, 6, 6
A_QK_DIM = HEAD_DIM // 2
ROPE_THETA = 500000.0
ROPE_FRACTION = 4
DILATED_BRANCHES = ((128, 1), (512, 4), (2048, 16))
MOBA_BLOCK = 256
MOBA_TOP_K = 3
MEM_HEADS = 4
MEM_WIDTH = MEM_HEADS * HEAD_DIM
D_FF = 2816
NORM_EPS = 1e-6
NEG = -1e30
LOG2E = math.log2(math.e)

LANES = 128
BF16_SUBLANES = 16
VMEM_LIMIT = 56 * 1024 * 1024

TM = 512
TM_PROJ = 1024
TM_POST = 1024
FF_CHUNK = 512
TQ = 256
TK = 256
TILE_INTERLEAVE = 8
N_GROUPS = D_MODEL // LANES
A_GROUPS, B_GROUPS, C_GROUPS = 2, 3, 3
HEADS_PER_GROUP = LANES // HEAD_DIM
VT_ROWS = HEAD_DIM + BF16_SUBLANES


def _group_kind(g):
    if g < A_GROUPS:
        return "A", A_QK_DIM ** -0.5 * LOG2E
    return "H", HEAD_DIM ** -0.5 * LOG2E


def _rms(x, g):
    return x * lax.rsqrt(jnp.mean(x * x, axis=-1, keepdims=True) + NORM_EPS) * g


def _resident(shape):
    nd = len(shape)
    return pl.BlockSpec(shape, lambda *_: (0,) * nd)


def _params(*sem):
    return pltpu.CompilerParams(dimension_semantics=sem, vmem_limit_bytes=VMEM_LIMIT)


def _ffn_body(*refs, final):
    if final:
        x_ref, g_ref, wg_ref, wu_ref, wd_ref, fg_ref, o_ref = refs
    else:
        x_ref, g_ref, wg_ref, wu_ref, wd_ref, o_ref = refs
    x = x_ref[...]
    h = _rms(x, g_ref[...]).astype(BF16)
    acc = jnp.zeros(x.shape, F32)
    for c0 in range(0, D_FF, FF_CHUNK):
        c1 = min(c0 + FF_CHUNK, D_FF)
        a = jnp.dot(h, wg_ref[:, c0:c1], preferred_element_type=F32)
        u = jnp.dot(h, wu_ref[:, c0:c1], preferred_element_type=F32)
        t = (a / (1.0 + jnp.exp(-a))) * u
        acc = acc + jnp.dot(t.astype(BF16), wd_ref[c0:c1, :], preferred_element_type=F32)
    y = x + 0.5 * acc
    if final:
        y = _rms(y, fg_ref[...])
    o_ref[...] = y


def _ffn(x, g, wg, wu, wd, final_g=None):
    t, d = x.shape
    final = final_g is not None
    in_specs = [
        pl.BlockSpec((TM, d), lambda i: (i, 0)),
        _resident((1, d)),
        _resident((d, D_FF)),
        _resident((d, D_FF)),
        _resident((D_FF, d)),
    ]
    args = [x, g, wg, wu, wd]
    if final:
        in_specs.append(_resident((1, d)))
        args.append(final_g)
    return pl.pallas_call(
        functools.partial(_ffn_body, final=final),
        out_shape=jax.ShapeDtypeStruct((t, d), F32),
        grid=(t // TM,),
        in_specs=in_specs,
        out_specs=pl.BlockSpec((TM, d), lambda i: (i, 0)),
        compiler_params=_params("parallel"),
        name="ffn_final" if final else "ffn",
    )(*args)


def _rotate_half(blk, first, half, axis):
    n = blk.shape[axis]
    return jnp.where(first, pltpu.roll(blk, n - half, axis), pltpu.roll(blk, half, axis))


def _wt_row(part, g):
    idx = "qkv".index(part)
    a_w, b_w, c_w = A_HEADS * HEAD_DIM, B_HEADS * HEAD_DIM, C_HEADS * HEAD_DIM
    if g < A_GROUPS:
        return idx * a_w + g * LANES
    if g < A_GROUPS + B_GROUPS:
        return 3 * a_w + idx * b_w + (g - A_GROUPS) * LANES
    return 3 * (a_w + b_w) + idx * c_w + (g - A_GROUPS - B_GROUPS) * LANES


def _proj_body(x_ref, g_ref, wt_ref, ca_ref, sa_ref, ch_ref, sh_ref, k_ref, qt_ref, vt_ref):
    h = _rms(x_ref[0], g_ref[...]).astype(BF16)
    tm = h.shape[0]
    nt = (((1,), (1,)), ((), ()))
    sub = lax.broadcasted_iota(jnp.int32, (LANES, tm), 0)
    tables = {}
    for kind, width, cos_ref, sin_ref in (("A", A_QK_DIM, ca_ref, sa_ref), ("H", HEAD_DIM, ch_ref, sh_ref)):
        rot = width // ROPE_FRACTION
        plain = width - rot
        reps = LANES // width
        cos_t = jnp.concatenate([cos_ref[...], jnp.ones((plain, tm), F32)] * reps, axis=0)
        sin_t = jnp.concatenate([sin_ref[...], jnp.zeros((plain, tm), F32)] * reps, axis=0)
        tables[kind] = (cos_t, sin_t, (sub & (width - 1)) < rot // 2, rot // 2)

    families = ((0, A_GROUPS), (A_GROUPS, A_GROUPS + B_GROUPS), (A_GROUPS + B_GROUPS, N_GROUPS))
    for part in "qkv":
        for g_lo, g_hi in families:
            r0 = _wt_row(part, g_lo)
            y = lax.dot_general(wt_ref[r0:r0 + (g_hi - g_lo) * LANES, :], h, nt,
                                preferred_element_type=F32)
            for g in range(g_lo, g_hi):
                blk = y[(g - g_lo) * LANES:(g - g_lo + 1) * LANES, :]
                cols = slice(g * LANES, (g + 1) * LANES)
                if part == "v":
                    vt_ref[0, cols, :] = blk.astype(BF16)
                    continue
                kind, scale = _group_kind(g)
                cos_t, sin_t, first, half = tables[kind]
                blk = blk * cos_t + _rotate_half(blk, first, half, 0) * sin_t
                if part == "q":
                    qt_ref[0, cols, :] = (blk * scale).astype(BF16)
                else:
                    k_ref[0, :, cols] = blk.T.astype(BF16)


def _proj(x, g, wt, tabs):
    b, s, d = x.shape
    per_seq = s // TM_PROJ
    tab = lambda rows: pl.BlockSpec((rows, TM_PROJ), lambda bi, i: (0, bi * per_seq + i))
    slab = pl.BlockSpec((1, d, TM_PROJ), lambda bi, i: (bi, 0, i))
    return pl.pallas_call(
        _proj_body,
        out_shape=(jax.ShapeDtypeStruct((b, s, d), BF16),
                   jax.ShapeDtypeStruct((b, d, s), BF16),
                   jax.ShapeDtypeStruct((b, d, s), BF16)),
        grid=(b, per_seq),
        in_specs=[pl.BlockSpec((1, TM_PROJ, d), lambda bi, i: (bi, i, 0)), _resident((1, d)), _resident(wt.shape)]
                 + [tab(t.shape[0]) for t in tabs],
        out_specs=(pl.BlockSpec((1, TM_PROJ, d), lambda bi, i: (bi, i, 0)), slab, slab),
        compiler_params=_params("parallel", "parallel"),
        name="mix_proj",
    )(x, g, wt, *tabs)


def _dilated_bias(offset):
    r = lax.broadcasted_iota(jnp.int32, (TK, TQ), 0)
    c = lax.broadcasted_iota(jnp.int32, (TK, TQ), 1)
    delta = c - r + offset * TK
    cnt = jnp.zeros((TK, TQ), jnp.int32)
    for window, dilation in DILATED_BRANCHES:
        hit = (delta <= window) & ((delta & (dilation - 1)) == 0)
        cnt = cnt + jnp.where(hit, 1, 0)
    cnt = jnp.where(delta >= 0, cnt, 0)
    return jnp.where(cnt == 3, math.log2(3.0),
                     jnp.where(cnt == 2, 1.0, jnp.where(cnt == 1, 0.0, NEG))).astype(F32)


N_BIAS = 4


def _attn_body(*refs, mode, nmaps, lam_init):
    if mode == "diff":
        qt_ref, k_ref, vt_ref, lam_ref, sg_ref, o_ref, qm_scr = refs
    elif mode == "moba":
        qt_ref, k_ref, vt_ref, o_ref, qm_scr, km_scr, sel_scr = refs
    else:
        qt_ref, k_ref, vt_ref, o_ref, qm_scr, bias_scr = refs
    nq = k_ref.shape[1] // TQ
    mcols = nmaps * TQ
    width = LANES // nmaps

    if mode == "dil":
        @pl.when((pl.program_id(0) == 0) & (pl.program_id(1) == 0))
        def _():
            for off in range(N_BIAS):
                bias_scr[off] = _dilated_bias(off)

    if mode == "moba":
        km_scr[...] = jnp.zeros(km_scr.shape, F32)
        for j in range(nq):
            kb = k_ref[0, j * MOBA_BLOCK:(j + 1) * MOBA_BLOCK, :].astype(F32)
            km_scr[j:j + 1, :] = jnp.mean(kb, axis=0, keepdims=True)
        km = km_scr[...]
        km_hi = km.astype(BF16)
        km_lo = (km - km_hi.astype(F32)).astype(BF16)

    def prepare(c):
        sub_q = lax.broadcasted_iota(jnp.int32, (LANES, TQ), 0)
        q32 = qt_ref[0, :, c * TQ:(c + 1) * TQ].astype(F32)
        for m in range(nmaps):
            keep = (sub_q >= m * width) & (sub_q < (m + 1) * width)
            qm_scr[c, :, m * TQ:(m + 1) * TQ] = jnp.where(keep, q32, 0.0).astype(BF16)
        if mode == "moba" and c > 0:
            qm = qm_scr[c]
            gate = (jnp.dot(km_hi, qm, preferred_element_type=F32)
                    + jnp.dot(km_lo, qm, preferred_element_type=F32))
            blk_id = lax.broadcasted_iota(jnp.int32, gate.shape, 0)
            rank = jnp.zeros(gate.shape, F32)
            for jp in range(c):
                rowj = gate[jp:jp + 1, :]
                beats = (rowj > gate) | ((rowj == gate) & (blk_id > jp))
                rank = rank + jnp.where(beats, 1.0, 0.0)
            sel_scr[c] = jnp.where((rank < MOBA_TOP_K - 0.5) & (blk_id < c), 1.0, 0.0)

    def finish(c, accs):
        heads = []
        for acc in accs:
            outs = [acc[0:HEAD_DIM, m * TQ:(m + 1) * TQ] / acc[HEAD_DIM:HEAD_DIM + 1, m * TQ:(m + 1) * TQ]
                    for m in range(nmaps // HEADS_PER_GROUP)]
            heads.append(outs[0] - lam_ref[0:1, 0:1] * outs[1] if mode == "diff" else outs[0])
        o = jnp.concatenate(heads, axis=0).T
        if mode == "diff":
            lane_o = lax.broadcasted_iota(jnp.int32, (TQ, LANES), 1)
            sq = o * o
            ss0 = jnp.sum(jnp.where(lane_o < HEAD_DIM, sq, 0.0), axis=1, keepdims=True)
            ss1 = jnp.sum(jnp.where(lane_o >= HEAD_DIM, sq, 0.0), axis=1, keepdims=True)
            var = jnp.where(lane_o < HEAD_DIM, ss0, ss1) * (1.0 / HEAD_DIM)
            o = o * lax.rsqrt(var + NORM_EPS) * sg_ref[...] * (1.0 - lam_init)
        o_ref[0, c * TQ:(c + 1) * TQ, :] = o.astype(BF16)

    ones_rows = jnp.ones((BF16_SUBLANES, TK), BF16)

    def scores(c, b):
        s = jnp.dot(k_ref[0, b * TK:(b + 1) * TK, :], qm_scr[c], preferred_element_type=F32)
        if mode == "dil":
            return s + jnp.concatenate([bias_scr[min(c - b, N_BIAS - 1)]] * nmaps, axis=1)
        if b == c:
            kpos = lax.broadcasted_iota(jnp.int32, (TK, mcols), 0)
            qpos = lax.broadcasted_iota(jnp.int32, (TK, mcols), 1) & (TQ - 1)
            return jnp.where(kpos <= qpos, s, NEG)
        return s

    def query_tile(c):
        order = [c] + list(range(c))
        s_next = scores(c, order[0])
        m = accs = None
        hcols = mcols // HEADS_PER_GROUP
        for idx, b in enumerate(order):
            yield
            s = s_next
            if idx + 1 < len(order):
                s_next = scores(c, order[idx + 1])
            top = jnp.max(s, axis=0, keepdims=True)
            gated = mode == "moba" and b < c
            if gated:
                chosen = sel_scr[c, b:b + 1, :] > 0.5
                top = jnp.where(chosen, top, NEG)
            m_new = top if m is None else jnp.maximum(m, top)
            sub = jnp.where(chosen, m_new, -NEG) if gated else m_new
            p = jnp.exp2(s - sub).astype(BF16)
            pvs = []
            for hd in range(HEADS_PER_GROUP):
                vt = jnp.concatenate([vt_ref[0, hd * HEAD_DIM:(hd + 1) * HEAD_DIM, b * TK:(b + 1) * TK], ones_rows],
                                     axis=0)
                pvs.append(jnp.dot(vt, p[:, hd * hcols:(hd + 1) * hcols], preferred_element_type=F32))
            if accs is None:
                accs = pvs
            else:
                alpha = jnp.exp2(m - m_new)
                accs = [alpha[:, hd * hcols:(hd + 1) * hcols] * accs[hd] + pvs[hd] for hd in range(HEADS_PER_GROUP)]
            m = m_new
        finish(c, accs)

    for c in range(nq):
        prepare(c)
    tiles = list(range(nq - 1, -1, -1))
    for g0 in range(0, nq, TILE_INTERLEAVE):
        live = [query_tile(c) for c in tiles[g0:g0 + TILE_INTERLEAVE]]
        while live:
            live = [gen for gen in live if next(gen, "done") != "done"]


def _attn(k, qt, vt, *, mode, g0, ngroups, extra=(), lam_init=0.0):
    b, s, _ = k.shape
    nmaps = 4 if mode == "diff" else 2
    mcols = nmaps * TQ
    in_specs = [
        pl.BlockSpec((1, LANES, s), lambda bi, g: (bi, g0 + g, 0)),
        pl.BlockSpec((1, s, LANES), lambda bi, g: (bi, 0, g0 + g)),
        pl.BlockSpec((1, LANES, s), lambda bi, g: (bi, g0 + g, 0)),
    ] + [_resident((1, LANES)) for _ in extra]
    nq = s // TQ
    scratch = [pltpu.VMEM((nq, LANES, mcols), BF16)]
    if mode == "moba":
        scratch += [pltpu.VMEM((BF16_SUBLANES, LANES), F32), pltpu.VMEM((nq, BF16_SUBLANES, mcols), F32)]
    if mode == "dil":
        scratch.append(pltpu.VMEM((N_BIAS, TK, TQ), F32))
    return pl.pallas_call(
        functools.partial(_attn_body, mode=mode, nmaps=nmaps, lam_init=lam_init),
        out_shape=jax.ShapeDtypeStruct((b, s, ngroups * LANES), BF16),
        grid=(b, ngroups),
        in_specs=in_specs,
        out_specs=pl.BlockSpec((1, s, LANES), lambda bi, g: (bi, 0, g)),
        scratch_shapes=scratch,
        compiler_params=_params("arbitrary", "arbitrary"),
        name="attn_" + mode,
    )(qt, k, vt, *extra)


def _memkv_body(m_ref, g_ref, w_ref, o_ref):
    h = _rms(m_ref[0], g_ref[...]).astype(BF16)
    o_ref[0] = jnp.dot(h, w_ref[...], preferred_element_type=F32).astype(BF16)


def _memkv(mem, g, w_kv):
    b, m, d = mem.shape
    n = w_kv.shape[1]
    return pl.pallas_call(
        _memkv_body,
        out_shape=jax.ShapeDtypeStruct((b, m, n), BF16),
        grid=(b,),
        in_specs=[pl.BlockSpec((1, m, d), lambda i: (i, 0, 0)), _resident((1, d)), _resident((d, n))],
        out_specs=pl.BlockSpec((1, m, n), lambda i: (i, 0, 0)),
        compiler_params=_params("parallel"),
        name="mem_kv",
    )(mem, g, w_kv)


def _post_body(x_ref, oa_ref, ob_ref, oc_ref, wout_ref, g_ref, wq_ref, kv_ref, wo_ref, o_ref):
    x = x_ref[...]
    r0 = 0
    for part_ref in (oa_ref, ob_ref, oc_ref):
        r1 = r0 + part_ref.shape[1]
        x = x + jnp.dot(part_ref[...], wout_ref[r0:r1, :], preferred_element_type=F32)
        r0 = r1

    h = _rms(x, g_ref[...]).astype(BF16)
    q = jnp.dot(h, wq_ref[...], preferred_element_type=F32) * (HEAD_DIM ** -0.5)
    tm = q.shape[0]
    lane = lax.broadcasted_iota(jnp.int32, (tm, LANES), 1)
    outs = []
    for gi in range(MEM_WIDTH // LANES):
        qg = q[:, gi * LANES:(gi + 1) * LANES]
        qm = jnp.concatenate([jnp.where(lane < HEAD_DIM, qg, 0.0),
                              jnp.where(lane >= HEAD_DIM, qg, 0.0)], axis=0).astype(BF16)
        kg = kv_ref[0, :, gi * LANES:(gi + 1) * LANES]
        vg = kv_ref[0, :, MEM_WIDTH + gi * LANES:MEM_WIDTH + (gi + 1) * LANES]
        s = lax.dot_general(qm, kg, (((1,), (1,)), ((), ())), preferred_element_type=F32)
        p = jnp.exp(s - jnp.max(s, axis=1, keepdims=True))
        l = jnp.sum(p, axis=1, keepdims=True)
        o = jnp.dot(p.astype(BF16), vg, preferred_element_type=F32) / l
        outs.append(jnp.where(lane < HEAD_DIM, o[0:tm], o[tm:2 * tm]).astype(BF16))
    o_mem = jnp.concatenate(outs, axis=1)
    o_ref[...] = x + jnp.dot(o_mem, wo_ref[...], preferred_element_type=F32)


def _post(x, oa, ob, oc, wout, g, wq, kv, wo, seq):
    t, d = x.shape
    tiles_per_seq = seq // TM_POST
    tile = lambda w: pl.BlockSpec((TM_POST, w), lambda i: (i, 0))
    m, n = kv.shape[1], kv.shape[2]
    return pl.pallas_call(
        _post_body,
        out_shape=jax.ShapeDtypeStruct((t, d), F32),
        grid=(t // TM_POST,),
        in_specs=[tile(d), tile(oa.shape[1]), tile(ob.shape[1]), tile(oc.shape[1]),
                  _resident(wout.shape),
                  _resident((1, d)), _resident(wq.shape),
                  pl.BlockSpec((1, m, n), lambda i: (i // tiles_per_seq, 0, 0)),
                  _resident(wo.shape)],
        out_specs=tile(d),
        compiler_params=_params("parallel"),
        name="mix_out_mem",
    )(x, oa, ob, oc, wout, g, wq, kv, wo)


def _rotary_tables(positions, head_width):
    rot = head_width // ROPE_FRACTION
    inv_freq = ROPE_THETA ** (-jnp.arange(0, rot, 2, dtype=F32) / rot)
    ang = inv_freq[:, None] * positions.astype(F32).reshape(1, -1)
    cos, sin = jnp.cos(ang), jnp.sin(ang)
    return jnp.concatenate([cos, cos], axis=0), jnp.concatenate([-sin, sin], axis=0)


def kernel(x, mem, positions, ffn1_norm, ffn1_w_gate, ffn1_w_up, ffn1_w_down, mix_norm, w_in, w_out,
           diff_lambda_q1, diff_lambda_k1, diff_lambda_q2, diff_lambda_k2, diff_subln,
           memq_norm, memkv_norm, mem_w_q, mem_w_kv, mem_w_o,
           ffn2_norm, ffn2_w_gate, ffn2_w_up, ffn2_w_down, final_norm):
    b, s, d = x.shape
    depth = w_in.shape[0]
    assert d == D_MODEL and s % TM_PROJ == 0 and s % TM_POST == 0 and (b * s) % TM == 0 and TQ == TK == MOBA_BLOCK
    assert s <= DILATED_BRANCHES[-1][0]

    tabs = _rotary_tables(positions, A_QK_DIM) + _rotary_tables(positions, HEAD_DIM)
    row = lambda v: v.reshape(1, -1).astype(F32)
    bf = lambda w: w.astype(BF16)

    xt = x.reshape(b * s, d)
    for l in range(depth):
        lam_init = 0.8 - 0.6 * math.exp(-0.3 * l)
        lam = (jnp.exp(jnp.sum(diff_lambda_q1[l].astype(F32) * diff_lambda_k1[l].astype(F32)))
               - jnp.exp(jnp.sum(diff_lambda_q2[l].astype(F32) * diff_lambda_k2[l].astype(F32)))
               + lam_init)
        lam_row = jnp.full((1, LANES), lam, F32)
        subln_row = jnp.tile(row(diff_subln[l]), (1, LANES // HEAD_DIM))

        xt = _ffn(xt, row(ffn1_norm[l]), bf(ffn1_w_gate[l]), bf(ffn1_w_up[l]), bf(ffn1_w_down[l]))

        k, qt, vt = _proj(xt.reshape(b, s, d), row(mix_norm[l]), bf(w_in[l].T), tabs)
        oa = _attn(k, qt, vt, mode="diff", g0=0, ngroups=A_GROUPS, extra=(lam_row, subln_row), lam_init=lam_init)
        ob = _attn(k, qt, vt, mode="dil", g0=A_GROUPS, ngroups=B_GROUPS)
        oc = _attn(k, qt, vt, mode="moba", g0=A_GROUPS + B_GROUPS, ngroups=C_GROUPS)

        kv = _memkv(mem, row(memkv_norm[l]), bf(mem_w_kv[l]))
        xt = _post(xt, oa.reshape(b * s, -1), ob.reshape(b * s, -1), oc.reshape(b * s, -1), bf(w_out[l]),
                   row(memq_norm[l]), bf(mem_w_q[l]), kv, bf(mem_w_o[l]), s)

        last = l == depth - 1
        xt = _ffn(xt, row(ffn2_norm[l]), bf(ffn2_w_gate[l]), bf(ffn2_w_up[l]), bf(ffn2_w_down[l]),
                  final_g=row(final_norm) if last else None)
    return xt.reshape(b, s, d)
```

```python
import functools
import math

import jax
import jax.numpy as jnp
from jax import lax
from jax.experimental import pallas as pl
from jax.experimental.pallas import tpu as pltpu

F32 = jnp.float32
BF16 = jnp.bfloat16

D_MODEL = 1024
HEAD_DIM = 64
A_HEADS, B_HEADS, C_HEADS = 4, 6, 6
A_QK_DIM = HEAD_DIM // 2
ROPE_THETA = 500000.0
ROPE_FRACTION = 4
DILATED_BRANCHES = ((128, 1), (512, 4), (2048, 16))
MOBA_BLOCK = 256
MOBA_TOP_K = 3
MEM_HEADS = 4
MEM_WIDTH = MEM_HEADS * HEAD_DIM
D_FF = 2816
NORM_EPS = 1e-6
NEG = -1e30
LOG2E = math.log2(math.e)

LANES = 128
BF16_SUBLANES = 16
VMEM_LIMIT = 56 * 1024 * 1024

TM = 512
TM_PROJ = 1024
TM_POST = 1024
FF_CHUNK = 512
CAST_SPLIT = 4
TQ = 256
TK = 256
TILE_INTERLEAVE = 8
N_GROUPS = D_MODEL // LANES
A_GROUPS, B_GROUPS, C_GROUPS = 2, 3, 3
HEADS_PER_GROUP = LANES // HEAD_DIM


def _group_kind(g):
    if g < A_GROUPS:
        return "A", A_QK_DIM ** -0.5 * LOG2E
    return "H", HEAD_DIM ** -0.5 * LOG2E


def _rms(x, g):
    return x * lax.rsqrt(jnp.mean(x * x, axis=-1, keepdims=True) + NORM_EPS) * g


def _resident(shape):
    nd = len(shape)
    return pl.BlockSpec(shape, lambda *_: (0,) * nd)


def _layer(shape, l):
    nd = len(shape)
    return pl.BlockSpec((None,) + tuple(shape), lambda *_: (l,) + (0,) * nd)


def _params(*sem):
    return pltpu.CompilerParams(dimension_semantics=sem, vmem_limit_bytes=VMEM_LIMIT)


def _ffn_body(*refs, final):
    if final:
        x_ref, g_ref, wg_ref, wu_ref, wd_ref, fg_ref, o_ref = refs
    else:
        x_ref, g_ref, wg_ref, wu_ref, wd_ref, o_ref = refs
    x = x_ref[...]
    h = _rms(x, g_ref[...]).astype(BF16)
    acc = jnp.zeros(x.shape, F32)
    for c0 in range(0, D_FF, FF_CHUNK):
        c1 = min(c0 + FF_CHUNK, D_FF)
        a = jnp.dot(h, wg_ref[:, c0:c1], preferred_element_type=F32)
        u = jnp.dot(h, wu_ref[:, c0:c1], preferred_element_type=F32)
        t = (a / (1.0 + jnp.exp(-a))) * u
        acc = acc + jnp.dot(t.astype(BF16), wd_ref[c0:c1, :], preferred_element_type=F32)
    y = x + 0.5 * acc
    if final:
        y = _rms(y, fg_ref[...])
    o_ref[...] = y


def _cast_body(w_ref, o_ref):
    o_ref[...] = w_ref[...].astype(BF16)


def _to_bf16(w):
    depth, r, c = w.shape
    blk = pl.BlockSpec((1, r // CAST_SPLIT, c), lambda l, j: (l, j, 0))
    return pl.pallas_call(
        _cast_body,
        out_shape=jax.ShapeDtypeStruct(w.shape, BF16),
        grid=(depth, CAST_SPLIT),
        in_specs=[blk],
        out_specs=blk,
        compiler_params=_params("parallel", "parallel"),
        name="to_bf16",
    )(w)


def _ffn(x, g, wg, wu, wd, l, final_g=None):
    t, d = x.shape
    final = final_g is not None
    in_specs = [
        pl.BlockSpec((TM, d), lambda i: (i, 0)),
        _resident((1, d)),
        _layer((d, D_FF), l),
        _layer((d, D_FF), l),
        _layer((D_FF, d), l),
    ]
    args = [x, g, wg, wu, wd]
    if final:
        in_specs.append(_resident((1, d)))
        args.append(final_g)
    return pl.pallas_call(
        functools.partial(_ffn_body, final=final),
        out_shape=jax.ShapeDtypeStruct((t, d), F32),
        grid=(t // TM,),
        in_specs=in_specs,
        out_specs=pl.BlockSpec((TM, d), lambda i: (i, 0)),
        compiler_params=_params("parallel"),
        name="ffn_final" if final else "ffn",
    )(*args)


def _rotate_half(blk, first, half, axis):
    n = blk.shape[axis]
    return jnp.where(first, pltpu.roll(blk, n - half, axis), pltpu.roll(blk, half, axis))


def _wt_row(part, g):
    idx = "qkv".index(part)
    a_w, b_w, c_w = A_HEADS * HEAD_DIM, B_HEADS * HEAD_DIM, C_HEADS * HEAD_DIM
    if g < A_GROUPS:
        return idx * a_w + g * LANES
    if g < A_GROUPS + B_GROUPS:
        return 3 * a_w + idx * b_w + (g - A_GROUPS) * LANES
    return 3 * (a_w + b_w) + idx * c_w + (g - A_GROUPS - B_GROUPS) * LANES


def _proj_body(x_ref, g_ref, wt_ref, ca_ref, sa_ref, ch_ref, sh_ref, k_ref, qt_ref, vt_ref):
    h = _rms(x_ref[0], g_ref[...]).astype(BF16)
    tm = h.shape[0]
    nt = (((1,), (1,)), ((), ()))
    sub = lax.broadcasted_iota(jnp.int32, (LANES, tm), 0)
    tables = {}
    for kind, width, cos_ref, sin_ref in (("A", A_QK_DIM, ca_ref, sa_ref), ("H", HEAD_DIM, ch_ref, sh_ref)):
        rot = width // ROPE_FRACTION
        plain = width - rot
        reps = LANES // width
        cos_t = jnp.concatenate([cos_ref[...], jnp.ones((plain, tm), F32)] * reps, axis=0)
        sin_t = jnp.concatenate([sin_ref[...], jnp.zeros((plain, tm), F32)] * reps, axis=0)
        tables[kind] = (cos_t, sin_t, (sub & (width - 1)) < rot // 2, rot // 2)

    families = ((0, A_GROUPS), (A_GROUPS, A_GROUPS + B_GROUPS), (A_GROUPS + B_GROUPS, N_GROUPS))
    for part in "qkv":
        for g_lo, g_hi in families:
            r0 = _wt_row(part, g_lo)
            y = lax.dot_general(wt_ref[r0:r0 + (g_hi - g_lo) * LANES, :], h, nt,
                                preferred_element_type=F32)
            for g in range(g_lo, g_hi):
                blk = y[(g - g_lo) * LANES:(g - g_lo + 1) * LANES, :]
                cols = slice(g * LANES, (g + 1) * LANES)
                if part == "v":
                    vt_ref[0, cols, :] = blk.astype(BF16)
                    continue
                kind, scale = _group_kind(g)
                cos_t, sin_t, first, half = tables[kind]
                blk = blk * cos_t + _rotate_half(blk, first, half, 0) * sin_t
                if part == "q":
                    qt_ref[0, cols, :] = (blk * scale).astype(BF16)
                else:
                    k_ref[0, :, cols] = blk.T.astype(BF16)


def _proj(x, g, wt, tabs):
    b, s, d = x.shape
    per_seq = s // TM_PROJ
    tab = lambda rows: pl.BlockSpec((rows, TM_PROJ), lambda bi, i: (0, bi * per_seq + i))
    slab = pl.BlockSpec((1, d, TM_PROJ), lambda bi, i: (bi, 0, i))
    return pl.pallas_call(
        _proj_body,
        out_shape=(jax.ShapeDtypeStruct((b, s, d), BF16),
                   jax.ShapeDtypeStruct((b, d, s), BF16),
                   jax.ShapeDtypeStruct((b, d, s), BF16)),
        grid=(b, per_seq),
        in_specs=[pl.BlockSpec((1, TM_PROJ, d), lambda bi, i: (bi, i, 0)), _resident((1, d)), _resident(wt.shape)]
                 + [tab(t.shape[0]) for t in tabs],
        out_specs=(pl.BlockSpec((1, TM_PROJ, d), lambda bi, i: (bi, i, 0)), slab, slab),
        compiler_params=_params("parallel", "parallel"),
        name="mix_proj",
    )(x, g, wt, *tabs)


def _dilated_bias(offset):
    r = lax.broadcasted_iota(jnp.int32, (TK, TQ), 0)
    c = lax.broadcasted_iota(jnp.int32, (TK, TQ), 1)
    delta = c - r + offset * TK
    cnt = jnp.zeros((TK, TQ), jnp.int32)
    for window, dilation in DILATED_BRANCHES:
        hit = (delta <= window) & ((delta & (dilation - 1)) == 0)
        cnt = cnt + jnp.where(hit, 1, 0)
    cnt = jnp.where(delta >= 0, cnt, 0)
    return jnp.where(cnt == 3, math.log2(3.0),
                     jnp.where(cnt == 2, 1.0, jnp.where(cnt == 1, 0.0, NEG))).astype(F32)


N_BIAS = 4


def _attn_body(*refs, mode, nmaps, lam_init):
    if mode == "diff":
        qt_ref, k_ref, vt_ref, lam_ref, sg_ref, o_ref, qm_scr = refs
    elif mode == "moba":
        qt_ref, k_ref, vt_ref, o_ref, qm_scr, km_scr, sel_scr = refs
    else:
        qt_ref, k_ref, vt_ref, o_ref, qm_scr, bias_scr = refs
    nq = k_ref.shape[1] // TQ
    mcols = nmaps * TQ
    width = LANES // nmaps

    if mode == "dil":
        @pl.when((pl.program_id(0) == 0) & (pl.program_id(1) == 0))
        def _():
            for off in range(N_BIAS):
                bias_scr[off] = _dilated_bias(off)

    if mode == "moba":
        km_scr[...] = jnp.zeros(km_scr.shape, F32)
        for j in range(nq):
            kb = k_ref[0, j * MOBA_BLOCK:(j + 1) * MOBA_BLOCK, :].astype(F32)
            km_scr[j:j + 1, :] = jnp.mean(kb, axis=0, keepdims=True)
        km = km_scr[...]
        km_hi = km.astype(BF16)
        km_lo = (km - km_hi.astype(F32)).astype(BF16)

    def prepare(c):
        sub_q = lax.broadcasted_iota(jnp.int32, (LANES, TQ), 0)
        q32 = qt_ref[0, :, c * TQ:(c + 1) * TQ].astype(F32)
        for m in range(nmaps):
            keep = (sub_q >= m * width) & (sub_q < (m + 1) * width)
            qm_scr[c, :, m * TQ:(m + 1) * TQ] = jnp.where(keep, q32, 0.0).astype(BF16)
        if mode == "moba" and c > 0:
            qm = qm_scr[c]
            gate = (jnp.dot(km_hi, qm, preferred_element_type=F32)
                    + jnp.dot(km_lo, qm, preferred_element_type=F32))
            blk_id = lax.broadcasted_iota(jnp.int32, gate.shape, 0)
            rank = jnp.zeros(gate.shape, F32)
            for jp in range(c):
                rowj = gate[jp:jp + 1, :]
                beats = (rowj > gate) | ((rowj == gate) & (blk_id > jp))
                rank = rank + jnp.where(beats, 1.0, 0.0)
            sel_scr[c] = jnp.where((rank < MOBA_TOP_K - 0.5) & (blk_id < c), 1.0, 0.0)

    pieces = 1 if mode == "diff" else HEADS_PER_GROUP
    heads_per_piece = HEADS_PER_GROUP // pieces
    vrows = heads_per_piece * HEAD_DIM
    pcols = mcols // pieces
    maps_per_head = nmaps // HEADS_PER_GROUP

    def finish(c, accs):
        heads = []
        for acc in accs:
            for hh in range(heads_per_piece):
                outs = []
                for mm in range(maps_per_head):
                    cols = slice((hh * maps_per_head + mm) * TQ, (hh * maps_per_head + mm + 1) * TQ)
                    outs.append(acc[hh * HEAD_DIM:(hh + 1) * HEAD_DIM, cols] / acc[vrows:vrows + 1, cols])
                heads.append(outs[0] - lam_ref[0:1, 0:1] * outs[1] if mode == "diff" else outs[0])
        o = jnp.concatenate(heads, axis=0).T
        if mode == "diff":
            lane_o = lax.broadcasted_iota(jnp.int32, (TQ, LANES), 1)
            sq = o * o
            ss0 = jnp.sum(jnp.where(lane_o < HEAD_DIM, sq, 0.0), axis=1, keepdims=True)
            ss1 = jnp.sum(jnp.where(lane_o >= HEAD_DIM, sq, 0.0), axis=1, keepdims=True)
            var = jnp.where(lane_o < HEAD_DIM, ss0, ss1) * (1.0 / HEAD_DIM)
            o = o * lax.rsqrt(var + NORM_EPS) * sg_ref[...] * (1.0 - lam_init)
        o_ref[0, c * TQ:(c + 1) * TQ, :] = o.astype(BF16)

    ones_rows = jnp.ones((BF16_SUBLANES, TK), BF16)

    def scores(c, b):
        s = jnp.dot(k_ref[0, b * TK:(b + 1) * TK, :], qm_scr[c], preferred_element_type=F32)
        if mode == "dil":
            return s + jnp.concatenate([bias_scr[min(c - b, N_BIAS - 1)]] * nmaps, axis=1)
        if b == c:
            kpos = lax.broadcasted_iota(jnp.int32, (TK, mcols), 0)
            qpos = lax.broadcasted_iota(jnp.int32, (TK, mcols), 1) & (TQ - 1)
            return jnp.where(kpos <= qpos, s, NEG)
        return s

    def query_tile(c):
        order = [c] + list(range(c))
        s_next = scores(c, order[0])
        m = accs = None
        for idx, b in enumerate(order):
            yield
            s = s_next
            if idx + 1 < len(order):
                s_next = scores(c, order[idx + 1])
            top = jnp.max(s, axis=0, keepdims=True)
            gated = mode == "moba" and b < c
            if gated:
                chosen = sel_scr[c, b:b + 1, :] > 0.5
                top = jnp.where(chosen, top, NEG)
            m_new = top if m is None else jnp.maximum(m, top)
            sub = jnp.where(chosen, m_new, -NEG) if gated else m_new
            p = jnp.exp2(s - sub).astype(BF16)
            pvs = []
            for pc in range(pieces):
                vt = jnp.concatenate([vt_ref[0, pc * vrows:(pc + 1) * vrows, b * TK:(b + 1) * TK], ones_rows],
                                     axis=0)
                pvs.append(jnp.dot(vt, p[:, pc * pcols:(pc + 1) * pcols], preferred_element_type=F32))
            if accs is None:
                accs = pvs
            else:
                alpha = jnp.exp2(m - m_new)
                accs = [alpha[:, pc * pcols:(pc + 1) * pcols] * accs[pc] + pvs[pc] for pc in range(pieces)]
            m = m_new
        finish(c, accs)

    for c in range(nq):
        prepare(c)
    tiles = list(range(nq - 1, -1, -1))
    for g0 in range(0, nq, TILE_INTERLEAVE):
        live = [query_tile(c) for c in tiles[g0:g0 + TILE_INTERLEAVE]]
        while live:
            live = [gen for gen in live if next(gen, "done") != "done"]


def _attn(k, qt, vt, *, mode, g0, ngroups, extra=(), lam_init=0.0):
    b, s, _ = k.shape
    nmaps = 4 if mode == "diff" else 2
    mcols = nmaps * TQ
    in_specs = [
        pl.BlockSpec((1, LANES, s), lambda bi, g: (bi, g0 + g, 0)),
        pl.BlockSpec((1, s, LANES), lambda bi, g: (bi, 0, g0 + g)),
        pl.BlockSpec((1, LANES, s), lambda bi, g: (bi, g0 + g, 0)),
    ] + [_resident((1, LANES)) for _ in extra]
    nq = s // TQ
    scratch = [pltpu.VMEM((nq, LANES, mcols), BF16)]
    if mode == "moba":
        scratch += [pltpu.VMEM((BF16_SUBLANES, LANES), F32), pltpu.VMEM((nq, BF16_SUBLANES, mcols), F32)]
    if mode == "dil":
        scratch.append(pltpu.VMEM((N_BIAS, TK, TQ), F32))
    return pl.pallas_call(
        functools.partial(_attn_body, mode=mode, nmaps=nmaps, lam_init=lam_init),
        out_shape=jax.ShapeDtypeStruct((b, s, ngroups * LANES), BF16),
        grid=(b, ngroups),
        in_specs=in_specs,
        out_specs=pl.BlockSpec((1, s, LANES), lambda bi, g: (bi, 0, g)),
        scratch_shapes=scratch,
        compiler_params=_params("arbitrary", "arbitrary"),
        name="attn_" + mode,
    )(qt, k, vt, *extra)


def _memkv_body(m_ref, g_ref, w_ref, o_ref):
    h = _rms(m_ref[0], g_ref[...]).astype(BF16)
    o_ref[0] = jnp.dot(h, w_ref[...], preferred_element_type=F32).astype(BF16)


def _memkv(mem, g, w_kv):
    b, m, d = mem.shape
    n = w_kv.shape[1]
    return pl.pallas_call(
        _memkv_body,
        out_shape=jax.ShapeDtypeStruct((b, m, n), BF16),
        grid=(b,),
        in_specs=[pl.BlockSpec((1, m, d), lambda i: (i, 0, 0)), _resident((1, d)), _resident((d, n))],
        out_specs=pl.BlockSpec((1, m, n), lambda i: (i, 0, 0)),
        compiler_params=_params("parallel"),
        name="mem_kv",
    )(mem, g, w_kv)


def _post_body(x_ref, oa_ref, ob_ref, oc_ref, wout_ref, g_ref, wq_ref, kv_ref, wo_ref, o_ref):
    x = x_ref[...]
    r0 = 0
    for part_ref in (oa_ref, ob_ref, oc_ref):
        r1 = r0 + part_ref.shape[1]
        x = x + jnp.dot(part_ref[...], wout_ref[r0:r1, :], preferred_element_type=F32)
        r0 = r1

    h = _rms(x, g_ref[...]).astype(BF16)
    q = jnp.dot(h, wq_ref[...], preferred_element_type=F32) * (HEAD_DIM ** -0.5)
    tm = q.shape[0]
    lane = lax.broadcasted_iota(jnp.int32, (tm, LANES), 1)
    outs = []
    for gi in range(MEM_WIDTH // LANES):
        qg = q[:, gi * LANES:(gi + 1) * LANES]
        qm = jnp.concatenate([jnp.where(lane < HEAD_DIM, qg, 0.0),
                              jnp.where(lane >= HEAD_DIM, qg, 0.0)], axis=0).astype(BF16)
        kg = kv_ref[0, :, gi * LANES:(gi + 1) * LANES]
        vg = kv_ref[0, :, MEM_WIDTH + gi * LANES:MEM_WIDTH + (gi + 1) * LANES]
        s = lax.dot_general(qm, kg, (((1,), (1,)), ((), ())), preferred_element_type=F32)
        p = jnp.exp(s - jnp.max(s, axis=1, keepdims=True))
        l = jnp.sum(p, axis=1, keepdims=True)
        o = jnp.dot(p.astype(BF16), vg, preferred_element_type=F32) / l
        outs.append(jnp.where(lane < HEAD_DIM, o[0:tm], o[tm:2 * tm]).astype(BF16))
    o_mem = jnp.concatenate(outs, axis=1)
    o_ref[...] = x + jnp.dot(o_mem, wo_ref[...], preferred_element_type=F32)


def _post(x, oa, ob, oc, wout, g, wq, kv, wo, seq):
    t, d = x.shape
    tiles_per_seq = seq // TM_POST
    tile = lambda w: pl.BlockSpec((TM_POST, w), lambda i: (i, 0))
    m, n = kv.shape[1], kv.shape[2]
    return pl.pallas_call(
        _post_body,
        out_shape=jax.ShapeDtypeStruct((t, d), F32),
        grid=(t // TM_POST,),
        in_specs=[tile(d), tile(oa.shape[1]), tile(ob.shape[1]), tile(oc.shape[1]),
                  _resident(wout.shape),
                  _resident((1, d)), _resident(wq.shape),
                  pl.BlockSpec((1, m, n), lambda i: (i // tiles_per_seq, 0, 0)),
                  _resident(wo.shape)],
        out_specs=tile(d),
        compiler_params=_params("parallel"),
        name="mix_out_mem",
    )(x, oa, ob, oc, wout, g, wq, kv, wo)


def _rotary_tables(positions, head_width):
    rot = head_width // ROPE_FRACTION
    inv_freq = ROPE_THETA ** (-jnp.arange(0, rot, 2, dtype=F32) / rot)
    ang = inv_freq[:, None] * positions.astype(F32).reshape(1, -1)
    cos, sin = jnp.cos(ang), jnp.sin(ang)
    return jnp.concatenate([cos, cos], axis=0), jnp.concatenate([-sin, sin], axis=0)


def kernel(x, mem, positions, ffn1_norm, ffn1_w_gate, ffn1_w_up, ffn1_w_down, mix_norm, w_in, w_out,
           diff_lambda_q1, diff_lambda_k1, diff_lambda_q2, diff_lambda_k2, diff_subln,
           memq_norm, memkv_norm, mem_w_q, mem_w_kv, mem_w_o,
           ffn2_norm, ffn2_w_gate, ffn2_w_up, ffn2_w_down, final_norm):
    b, s, d = x.shape
    depth = w_in.shape[0]
    assert d == D_MODEL and s % TM_PROJ == 0 and s % TM_POST == 0 and (b * s) % TM == 0 and TQ == TK == MOBA_BLOCK
    assert s <= DILATED_BRANCHES[-1][0]

    tabs = _rotary_tables(positions, A_QK_DIM) + _rotary_tables(positions, HEAD_DIM)
    row = lambda v: v.reshape(1, -1).astype(F32)
    bf = lambda w: w.astype(BF16)

    ffn1_w = [_to_bf16(w) for w in (ffn1_w_gate, ffn1_w_up, ffn1_w_down)]
    ffn2_w = [_to_bf16(w) for w in (ffn2_w_gate, ffn2_w_up, ffn2_w_down)]

    xt = x.reshape(b * s, d)
    for l in range(depth):
        lam_init = 0.8 - 0.6 * math.exp(-0.3 * l)
        lam = (jnp.exp(jnp.sum(diff_lambda_q1[l].astype(F32) * diff_lambda_k1[l].astype(F32)))
               - jnp.exp(jnp.sum(diff_lambda_q2[l].astype(F32) * diff_lambda_k2[l].astype(F32)))
               + lam_init)
        lam_row = jnp.full((1, LANES), lam, F32)
        subln_row = jnp.tile(row(diff_subln[l]), (1, LANES // HEAD_DIM))

        xt = _ffn(xt, row(ffn1_norm[l]), *ffn1_w, l)

        k, qt, vt = _proj(xt.reshape(b, s, d), row(mix_norm[l]), bf(w_in[l].T), tabs)
        oa = _attn(k, qt, vt, mode="diff", g0=0, ngroups=A_GROUPS, extra=(lam_row, subln_row), lam_init=lam_init)
        ob = _attn(k, qt, vt, mode="dil", g0=A_GROUPS, ngroups=B_GROUPS)
        oc = _attn(k, qt, vt, mode="moba", g0=A_GROUPS + B_GROUPS, ngroups=C_GROUPS)

        kv = _memkv(mem, row(memkv_norm[l]), bf(mem_w_kv[l]))
        xt = _post(xt, oa.reshape(b * s, -1), ob.reshape(b * s, -1), oc.reshape(b * s, -1), bf(w_out[l]),
                   row(memq_norm[l]), bf(mem_w_q[l]), kv, bf(mem_w_o[l]), s)

        last = l == depth - 1
        xt = _ffn(xt, row(ffn2_norm[l]), *ffn2_w, l, final_g=row(final_norm) if last else None)
    return xt.reshape(b, s, d)
```

```python
import functools
import math

import jax
import jax.numpy as jnp
from jax import lax
from jax.experimental import pallas as pl
from jax.experimental.pallas import tpu as pltpu

F32 = jnp.float32
BF16 = jnp.bfloat16

D_MODEL = 1024
HEAD_DIM = 64
A_HEADS, B_HEADS, C_HEADS = 4, 6, 6
A_QK_DIM = HEAD_DIM // 2
ROPE_THETA = 500000.0
ROPE_FRACTION = 4
DILATED_BRANCHES = ((128, 1), (512, 4), (2048, 16))
MOBA_BLOCK = 256
MOBA_TOP_K = 3
MEM_HEADS = 4
MEM_WIDTH = MEM_HEADS * HEAD_DIM
D_FF = 2816
NORM_EPS = 1e-6
NEG = -1e30
LOG2E = math.log2(math.e)

LANES = 128
BF16_SUBLANES = 16
VMEM_LIMIT = 56 * 1024 * 1024

TM = 512
TM_PROJ = 1024
TM_POST = 1024
FF_CHUNK = 512
FF_HEAD_SPLIT = 2
CAST_SPLIT = 4
TQ = 256
TK = 256
TILE_INTERLEAVE = 8
N_GROUPS = D_MODEL // LANES
A_GROUPS, B_GROUPS, C_GROUPS = 2, 3, 3
HEADS_PER_GROUP = LANES // HEAD_DIM


def _group_kind(g):
    if g < A_GROUPS:
        return "A", A_QK_DIM ** -0.5 * LOG2E
    return "H", HEAD_DIM ** -0.5 * LOG2E


def _rms(x, g):
    return x * lax.rsqrt(jnp.mean(x * x, axis=-1, keepdims=True) + NORM_EPS) * g


def _resident(shape):
    nd = len(shape)
    return pl.BlockSpec(shape, lambda *_: (0,) * nd)


def _layer(shape, l):
    nd = len(shape)
    return pl.BlockSpec((None,) + tuple(shape), lambda *_: (l,) + (0,) * nd)


def _params(*sem):
    return pltpu.CompilerParams(dimension_semantics=sem, vmem_limit_bytes=VMEM_LIMIT)


def _ffn_body(*refs, final):
    if final:
        x_ref, g_ref, wg_ref, wu_ref, wd_ref, fg_ref, o_ref = refs
    else:
        x_ref, g_ref, wg_ref, wu_ref, wd_ref, o_ref = refs
    x = x_ref[...]
    rows = x.shape[0] // FF_HEAD_SPLIT
    hs = [_rms(x[r * rows:(r + 1) * rows], g_ref[...]).astype(BF16) for r in range(FF_HEAD_SPLIT)]
    h = jnp.concatenate(hs, axis=0)
    acc = jnp.zeros(x.shape, F32)
    for c0 in range(0, D_FF, FF_CHUNK):
        c1 = min(c0 + FF_CHUNK, D_FF)
        if c0 == 0:
            a = jnp.concatenate([jnp.dot(hr, wg_ref[:, c0:c1], preferred_element_type=F32) for hr in hs], axis=0)
            u = jnp.concatenate([jnp.dot(hr, wu_ref[:, c0:c1], preferred_element_type=F32) for hr in hs], axis=0)
        else:
            a = jnp.dot(h, wg_ref[:, c0:c1], preferred_element_type=F32)
            u = jnp.dot(h, wu_ref[:, c0:c1], preferred_element_type=F32)
        t = (a / (1.0 + jnp.exp(-a))) * u
        acc = acc + jnp.dot(t.astype(BF16), wd_ref[c0:c1, :], preferred_element_type=F32)
    y = x + 0.5 * acc
    if final:
        y = _rms(y, fg_ref[...])
    o_ref[...] = y


def _cast_body(w_ref, o_ref):
    o_ref[...] = w_ref[...].astype(BF16)


def _to_bf16(w):
    depth, r, c = w.shape
    blk = pl.BlockSpec((1, r // CAST_SPLIT, c), lambda l, j: (l, j, 0))
    return pl.pallas_call(
        _cast_body,
        out_shape=jax.ShapeDtypeStruct(w.shape, BF16),
        grid=(depth, CAST_SPLIT),
        in_specs=[blk],
        out_specs=blk,
        compiler_params=_params("parallel", "parallel"),
        name="to_bf16",
    )(w)


def _ffn(x, g, wg, wu, wd, l, final_g=None):
    t, d = x.shape
    final = final_g is not None
    in_specs = [
        pl.BlockSpec((TM, d), lambda i: (i, 0)),
        _resident((1, d)),
        _layer((d, D_FF), l),
        _layer((d, D_FF), l),
        _layer((D_FF, d), l),
    ]
    args = [x, g, wg, wu, wd]
    if final:
        in_specs.append(_resident((1, d)))
        args.append(final_g)
    return pl.pallas_call(
        functools.partial(_ffn_body, final=final),
        out_shape=jax.ShapeDtypeStruct((t, d), F32),
        grid=(t // TM,),
        in_specs=in_specs,
        out_specs=pl.BlockSpec((TM, d), lambda i: (i, 0)),
        compiler_params=_params("parallel"),
        name="ffn_final" if final else "ffn",
    )(*args)


def _rotate_half(blk, first, half, axis):
    n = blk.shape[axis]
    return jnp.where(first, pltpu.roll(blk, n - half, axis), pltpu.roll(blk, half, axis))


def _wt_row(part, g):
    idx = "qkv".index(part)
    a_w, b_w, c_w = A_HEADS * HEAD_DIM, B_HEADS * HEAD_DIM, C_HEADS * HEAD_DIM
    if g < A_GROUPS:
        return idx * a_w + g * LANES
    if g < A_GROUPS + B_GROUPS:
        return 3 * a_w + idx * b_w + (g - A_GROUPS) * LANES
    return 3 * (a_w + b_w) + idx * c_w + (g - A_GROUPS - B_GROUPS) * LANES


def _proj_body(x_ref, g_ref, wt_ref, ca_ref, sa_ref, ch_ref, sh_ref, k_ref, qt_ref, vt_ref):
    h = _rms(x_ref[0], g_ref[...]).astype(BF16)
    tm = h.shape[0]
    nt = (((1,), (1,)), ((), ()))
    sub = lax.broadcasted_iota(jnp.int32, (LANES, tm), 0)
    tables = {}
    for kind, width, cos_ref, sin_ref in (("A", A_QK_DIM, ca_ref, sa_ref), ("H", HEAD_DIM, ch_ref, sh_ref)):
        rot = width // ROPE_FRACTION
        plain = width - rot
        reps = LANES // width
        cos_t = jnp.concatenate([cos_ref[...], jnp.ones((plain, tm), F32)] * reps, axis=0)
        sin_t = jnp.concatenate([sin_ref[...], jnp.zeros((plain, tm), F32)] * reps, axis=0)
        tables[kind] = (cos_t, sin_t, (sub & (width - 1)) < rot // 2, rot // 2)

    families = ((0, A_GROUPS), (A_GROUPS, A_GROUPS + B_GROUPS), (A_GROUPS + B_GROUPS, N_GROUPS))
    for part in "qkv":
        for g_lo, g_hi in families:
            r0 = _wt_row(part, g_lo)
            y = lax.dot_general(wt_ref[r0:r0 + (g_hi - g_lo) * LANES, :], h, nt,
                                preferred_element_type=F32)
            for g in range(g_lo, g_hi):
                blk = y[(g - g_lo) * LANES:(g - g_lo + 1) * LANES, :]
                cols = slice(g * LANES, (g + 1) * LANES)
                if part == "v":
                    vt_ref[0, cols, :] = blk.astype(BF16)
                    continue
                kind, scale = _group_kind(g)
                cos_t, sin_t, first, half = tables[kind]
                blk = blk * cos_t + _rotate_half(blk, first, half, 0) * sin_t
                if part == "q":
                    qt_ref[0, cols, :] = (blk * scale).astype(BF16)
                else:
                    k_ref[0, :, cols] = blk.T.astype(BF16)


def _proj(x, g, wt, tabs):
    b, s, d = x.shape
    per_seq = s // TM_PROJ
    tab = lambda rows: pl.BlockSpec((rows, TM_PROJ), lambda bi, i: (0, bi * per_seq + i))
    slab = pl.BlockSpec((1, d, TM_PROJ), lambda bi, i: (bi, 0, i))
    return pl.pallas_call(
        _proj_body,
        out_shape=(jax.ShapeDtypeStruct((b, s, d), BF16),
                   jax.ShapeDtypeStruct((b, d, s), BF16),
                   jax.ShapeDtypeStruct((b, d, s), BF16)),
        grid=(b, per_seq),
        in_specs=[pl.BlockSpec((1, TM_PROJ, d), lambda bi, i: (bi, i, 0)), _resident((1, d)), _resident(wt.shape)]
                 + [tab(t.shape[0]) for t in tabs],
        out_specs=(pl.BlockSpec((1, TM_PROJ, d), lambda bi, i: (bi, i, 0)), slab, slab),
        compiler_params=_params("parallel", "parallel"),
        name="mix_proj",
    )(x, g, wt, *tabs)


def _dilated_bias(offset):
    r = lax.broadcasted_iota(jnp.int32, (TK, TQ), 0)
    c = lax.broadcasted_iota(jnp.int32, (TK, TQ), 1)
    delta = c - r + offset * TK
    cnt = jnp.zeros((TK, TQ), jnp.int32)
    for window, dilation in DILATED_BRANCHES:
        hit = (delta <= window) & ((delta & (dilation - 1)) == 0)
        cnt = cnt + jnp.where(hit, 1, 0)
    cnt = jnp.where(delta >= 0, cnt, 0)
    return jnp.where(cnt == 3, math.log2(3.0),
                     jnp.where(cnt == 2, 1.0, jnp.where(cnt == 1, 0.0, NEG))).astype(F32)


N_BIAS = 4


def _attn_body(*refs, mode, nmaps, lam_init):
    if mode == "diff":
        qt_ref, k_ref, vt_ref, lam_ref, sg_ref, o_ref, qm_scr = refs
    elif mode == "moba":
        qt_ref, k_ref, vt_ref, o_ref, qm_scr, km_scr, sel_scr = refs
    else:
        qt_ref, k_ref, vt_ref, o_ref, qm_scr, bias_scr = refs
    nq = k_ref.shape[1] // TQ
    mcols = nmaps * TQ
    width = LANES // nmaps

    if mode == "dil":
        @pl.when((pl.program_id(0) == 0) & (pl.program_id(1) == 0))
        def _():
            for off in range(N_BIAS):
                bias_scr[off] = _dilated_bias(off)

    if mode == "moba":
        km_scr[...] = jnp.zeros(km_scr.shape, F32)
        for j in range(nq):
            kb = k_ref[0, j * MOBA_BLOCK:(j + 1) * MOBA_BLOCK, :].astype(F32)
            km_scr[j:j + 1, :] = jnp.mean(kb, axis=0, keepdims=True)
        km = km_scr[...]
        km_hi = km.astype(BF16)
        km_lo = (km - km_hi.astype(F32)).astype(BF16)

    def prepare(c):
        sub_q = lax.broadcasted_iota(jnp.int32, (LANES, TQ), 0)
        q32 = qt_ref[0, :, c * TQ:(c + 1) * TQ].astype(F32)
        for m in range(nmaps):
            keep = (sub_q >= m * width) & (sub_q < (m + 1) * width)
            qm_scr[c, :, m * TQ:(m + 1) * TQ] = jnp.where(keep, q32, 0.0).astype(BF16)
        if mode == "moba" and c > 0:
            qm = qm_scr[c]
            gate = (jnp.dot(km_hi, qm, preferred_element_type=F32)
                    + jnp.dot(km_lo, qm, preferred_element_type=F32))
            blk_id = lax.broadcasted_iota(jnp.int32, gate.shape, 0)
            rank = jnp.zeros(gate.shape, F32)
            for jp in range(c):
                rowj = gate[jp:jp + 1, :]
                beats = (rowj > gate) | ((rowj == gate) & (blk_id > jp))
                rank = rank + jnp.where(beats, 1.0, 0.0)
            sel_scr[c] = jnp.where((rank < MOBA_TOP_K - 0.5) & (blk_id < c), 1.0, 0.0)

    pieces = 1 if mode == "diff" else HEADS_PER_GROUP
    heads_per_piece = HEADS_PER_GROUP // pieces
    vrows = heads_per_piece * HEAD_DIM
    pcols = mcols // pieces
    maps_per_head = nmaps // HEADS_PER_GROUP

    def finish(c, accs):
        heads = []
        for acc in accs:
            for hh in range(heads_per_piece):
                outs = []
                for mm in range(maps_per_head):
                    cols = slice((hh * maps_per_head + mm) * TQ, (hh * maps_per_head + mm + 1) * TQ)
                    outs.append(acc[hh * HEAD_DIM:(hh + 1) * HEAD_DIM, cols] / acc[vrows:vrows + 1, cols])
                heads.append(outs[0] - lam_ref[0:1, 0:1] * outs[1] if mode == "diff" else outs[0])
        o = jnp.concatenate(heads, axis=0).T
        if mode == "diff":
            lane_o = lax.broadcasted_iota(jnp.int32, (TQ, LANES), 1)
            sq = o * o
            ss0 = jnp.sum(jnp.where(lane_o < HEAD_DIM, sq, 0.0), axis=1, keepdims=True)
            ss1 = jnp.sum(jnp.where(lane_o >= HEAD_DIM, sq, 0.0), axis=1, keepdims=True)
            var = jnp.where(lane_o < HEAD_DIM, ss0, ss1) * (1.0 / HEAD_DIM)
            o = o * lax.rsqrt(var + NORM_EPS) * sg_ref[...] * (1.0 - lam_init)
        o_ref[0, c * TQ:(c + 1) * TQ, :] = o.astype(BF16)

    ones_rows = jnp.ones((BF16_SUBLANES, TK), BF16)

    def scores(c, b):
        s = jnp.dot(k_ref[0, b * TK:(b + 1) * TK, :], qm_scr[c], preferred_element_type=F32)
        if mode == "dil":
            return s + jnp.concatenate([bias_scr[min(c - b, N_BIAS - 1)]] * nmaps, axis=1)
        if b == c:
            kpos = lax.broadcasted_iota(jnp.int32, (TK, mcols), 0)
            qpos = lax.broadcasted_iota(jnp.int32, (TK, mcols), 1) & (TQ - 1)
            return jnp.where(kpos <= qpos, s, NEG)
        return s

    def query_tile(c):
        order = [c] + list(range(c))
        s_next = scores(c, order[0])
        m = accs = None
        for idx, b in enumerate(order):
            yield
            s = s_next
            if idx + 1 < len(order):
                s_next = scores(c, order[idx + 1])
            top = jnp.max(s, axis=0, keepdims=True)
            gated = mode == "moba" and b < c
            if gated:
                chosen = sel_scr[c, b:b + 1, :] > 0.5
                top = jnp.where(chosen, top, NEG)
            m_new = top if m is None else jnp.maximum(m, top)
            sub = jnp.where(chosen, m_new, -NEG) if gated else m_new
            p = jnp.exp2(s - sub).astype(BF16)
            pvs = []
            for pc in range(pieces):
                vt = jnp.concatenate([vt_ref[0, pc * vrows:(pc + 1) * vrows, b * TK:(b + 1) * TK], ones_rows],
                                     axis=0)
                pvs.append(jnp.dot(vt, p[:, pc * pcols:(pc + 1) * pcols], preferred_element_type=F32))
            if accs is None:
                accs = pvs
            else:
                alpha = jnp.exp2(m - m_new)
                accs = [alpha[:, pc * pcols:(pc + 1) * pcols] * accs[pc] + pvs[pc] for pc in range(pieces)]
            m = m_new
        finish(c, accs)

    for c in range(nq):
        prepare(c)
    tiles = list(range(nq - 1, -1, -1))
    for g0 in range(0, nq, TILE_INTERLEAVE):
        live = [query_tile(c) for c in tiles[g0:g0 + TILE_INTERLEAVE]]
        while live:
            live = [gen for gen in live if next(gen, "done") != "done"]


def _attn(k, qt, vt, *, mode, g0, ngroups, extra=(), lam_init=0.0):
    b, s, _ = k.shape
    nmaps = 4 if mode == "diff" else 2
    mcols = nmaps * TQ
    in_specs = [
        pl.BlockSpec((1, LANES, s), lambda bi, g: (bi, g0 + g, 0)),
        pl.BlockSpec((1, s, LANES), lambda bi, g: (bi, 0, g0 + g)),
        pl.BlockSpec((1, LANES, s), lambda bi, g: (bi, g0 + g, 0)),
    ] + [_resident((1, LANES)) for _ in extra]
    nq = s // TQ
    scratch = [pltpu.VMEM((nq, LANES, mcols), BF16)]
    if mode == "moba":
        scratch += [pltpu.VMEM((BF16_SUBLANES, LANES), F32), pltpu.VMEM((nq, BF16_SUBLANES, mcols), F32)]
    if mode == "dil":
        scratch.append(pltpu.VMEM((N_BIAS, TK, TQ), F32))
    return pl.pallas_call(
        functools.partial(_attn_body, mode=mode, nmaps=nmaps, lam_init=lam_init),
        out_shape=jax.ShapeDtypeStruct((b, s, ngroups * LANES), BF16),
        grid=(b, ngroups),
        in_specs=in_specs,
        out_specs=pl.BlockSpec((1, s, LANES), lambda bi, g: (bi, 0, g)),
        scratch_shapes=scratch,
        compiler_params=_params("arbitrary", "arbitrary"),
        name="attn_" + mode,
    )(qt, k, vt, *extra)


def _memkv_body(m_ref, g_ref, w_ref, o_ref):
    h = _rms(m_ref[0], g_ref[...]).astype(BF16)
    o_ref[0] = jnp.dot(h, w_ref[...], preferred_element_type=F32).astype(BF16)


def _memkv(mem, g, w_kv):
    b, m, d = mem.shape
    n = w_kv.shape[1]
    return pl.pallas_call(
        _memkv_body,
        out_shape=jax.ShapeDtypeStruct((b, m, n), BF16),
        grid=(b,),
        in_specs=[pl.BlockSpec((1, m, d), lambda i: (i, 0, 0)), _resident((1, d)), _resident((d, n))],
        out_specs=pl.BlockSpec((1, m, n), lambda i: (i, 0, 0)),
        compiler_params=_params("parallel"),
        name="mem_kv",
    )(mem, g, w_kv)


def _post_body(x_ref, oa_ref, ob_ref, oc_ref, wout_ref, g_ref, wq_ref, kv_ref, wo_ref, o_ref):
    x = x_ref[...]
    r0 = 0
    for part_ref in (oa_ref, ob_ref, oc_ref):
        r1 = r0 + part_ref.shape[1]
        x = x + jnp.dot(part_ref[...], wout_ref[r0:r1, :], preferred_element_type=F32)
        r0 = r1

    h = _rms(x, g_ref[...]).astype(BF16)
    q = jnp.dot(h, wq_ref[...], preferred_element_type=F32) * (HEAD_DIM ** -0.5)
    tm = q.shape[0]
    lane = lax.broadcasted_iota(jnp.int32, (tm, LANES), 1)
    outs = []
    for gi in range(MEM_WIDTH // LANES):
        qg = q[:, gi * LANES:(gi + 1) * LANES]
        qm = jnp.concatenate([jnp.where(lane < HEAD_DIM, qg, 0.0),
                              jnp.where(lane >= HEAD_DIM, qg, 0.0)], axis=0).astype(BF16)
        kg = kv_ref[0, :, gi * LANES:(gi + 1) * LANES]
        vg = kv_ref[0, :, MEM_WIDTH + gi * LANES:MEM_WIDTH + (gi + 1) * LANES]
        s = lax.dot_general(qm, kg, (((1,), (1,)), ((), ())), preferred_element_type=F32)
        p = jnp.exp(s - jnp.max(s, axis=1, keepdims=True))
        l = jnp.sum(p, axis=1, keepdims=True)
        o = jnp.dot(p.astype(BF16), vg, preferred_element_type=F32) / l
        outs.append(jnp.where(lane < HEAD_DIM, o[0:tm], o[tm:2 * tm]).astype(BF16))
    o_mem = jnp.concatenate(outs, axis=1)
    o_ref[...] = x + jnp.dot(o_mem, wo_ref[...], preferred_element_type=F32)


def _post(x, oa, ob, oc, wout, g, wq, kv, wo, seq):
    t, d = x.shape
    tiles_per_seq = seq // TM_POST
    tile = lambda w: pl.BlockSpec((TM_POST, w), lambda i: (i, 0))
    m, n = kv.shape[1], kv.shape[2]
    return pl.pallas_call(
        _post_body,
        out_shape=jax.ShapeDtypeStruct((t, d), F32),
        grid=(t // TM_POST,),
        in_specs=[tile(d), tile(oa.shape[1]), tile(ob.shape[1]), tile(oc.shape[1]),
                  _resident(wout.shape),
                  _resident((1, d)), _resident(wq.shape),
                  pl.BlockSpec((1, m, n), lambda i: (i // tiles_per_seq, 0, 0)),
                  _resident(wo.shape)],
        out_specs=tile(d),
        compiler_params=_params("parallel"),
        name="mix_out_mem",
    )(x, oa, ob, oc, wout, g, wq, kv, wo)


def _rotary_tables(positions, head_width):
    rot = head_width // ROPE_FRACTION
    inv_freq = ROPE_THETA ** (-jnp.arange(0, rot, 2, dtype=F32) / rot)
    ang = inv_freq[:, None] * positions.astype(F32).reshape(1, -1)
    cos, sin = jnp.cos(ang), jnp.sin(ang)
    return jnp.concatenate([cos, cos], axis=0), jnp.concatenate([-sin, sin], axis=0)


def kernel(x, mem, positions, ffn1_norm, ffn1_w_gate, ffn1_w_up, ffn1_w_down, mix_norm, w_in, w_out,
           diff_lambda_q1, diff_lambda_k1, diff_lambda_q2, diff_lambda_k2, diff_subln,
           memq_norm, memkv_norm, mem_w_q, mem_w_kv, mem_w_o,
           ffn2_norm, ffn2_w_gate, ffn2_w_up, ffn2_w_down, final_norm):
    b, s, d = x.shape
    depth = w_in.shape[0]
    assert d == D_MODEL and s % TM_PROJ == 0 and s % TM_POST == 0 and (b * s) % TM == 0 and TQ == TK == MOBA_BLOCK
    assert s <= DILATED_BRANCHES[-1][0]

    tabs = _rotary_tables(positions, A_QK_DIM) + _rotary_tables(positions, HEAD_DIM)
    row = lambda v: v.reshape(1, -1).astype(F32)
    bf = lambda w: w.astype(BF16)

    ffn1_w = [_to_bf16(w) for w in (ffn1_w_gate, ffn1_w_up, ffn1_w_down)]
    ffn2_w = [_to_bf16(w) for w in (ffn2_w_gate, ffn2_w_up, ffn2_w_down)]
    w_in_bf = _to_bf16(w_in)

    xt = x.reshape(b * s, d)
    for l in range(depth):
        lam_init = 0.8 - 0.6 * math.exp(-0.3 * l)
        lam = (jnp.exp(jnp.sum(diff_lambda_q1[l].astype(F32) * diff_lambda_k1[l].astype(F32)))
               - jnp.exp(jnp.sum(diff_lambda_q2[l].astype(F32) * diff_lambda_k2[l].astype(F32)))
               + lam_init)
        lam_row = jnp.full((1, LANES), lam, F32)
        subln_row = jnp.tile(row(diff_subln[l]), (1, LANES // HEAD_DIM))

        xt = _ffn(xt, row(ffn1_norm[l]), *ffn1_w, l)

        k, qt, vt = _proj(xt.reshape(b, s, d), row(mix_norm[l]), w_in_bf[l].T, tabs)
        oa = _attn(k, qt, vt, mode="diff", g0=0, ngroups=A_GROUPS, extra=(lam_row, subln_row), lam_init=lam_init)
        ob = _attn(k, qt, vt, mode="dil", g0=A_GROUPS, ngroups=B_GROUPS)
        oc = _attn(k, qt, vt, mode="moba", g0=A_GROUPS + B_GROUPS, ngroups=C_GROUPS)

        kv = _memkv(mem, row(memkv_norm[l]), bf(mem_w_kv[l]))
        xt = _post(xt, oa.reshape(b * s, -1), ob.reshape(b * s, -1), oc.reshape(b * s, -1), bf(w_out[l]),
                   row(memq_norm[l]), bf(mem_w_q[l]), kv, bf(mem_w_o[l]), s)

        last = l == depth - 1
        xt = _ffn(xt, row(ffn2_norm[l]), *ffn2_w, l, final_g=row(final_norm) if last else None)
    return xt.reshape(b, s, d)
```
